```python
import math
import jax
import jax.numpy as jnp
from jax import lax
import numpy as np

D_MODEL = 1024
BATCH = 4
SEQ = 8192
DEPTH = 1
DEC_BATCH = 32
DEC_SEQ = 8
PAST_LEN = 16384
PAGE_SIZE = 128

N_Q_HEADS = D_MODEL // 128
N_KV_HEADS = max(1, N_Q_HEADS // 4)
GROUP = N_Q_HEADS // N_KV_HEADS
HEAD_DIM = 64
D_ATTN = N_Q_HEADS * HEAD_DIM
CMP_BLOCK = 32
CMP_STRIDE = 16
CMP_RATIO = CMP_BLOCK // CMP_STRIDE
SEL_BLOCK = 64
N_SEL = 16
WINDOW = 512
Q_BLOCK = 128
N_KV_SLOTS = 6
D_SSM = D_MODEL // 2
SSM_GROUP = 16
N_SSM_GROUPS = D_SSM // SSM_GROUP
SSM_STATE = 64
SSM_CHUNK = 256
DT_MIN = 0.001
DT_MAX = 0.1
RMS_EPS = 1e-6
PROJ_SPLITS = (D_ATTN, N_KV_SLOTS * N_KV_HEADS * HEAD_DIM, 3 * N_Q_HEADS, D_ATTN, D_SSM, D_SSM, 2 * D_MODEL)
D_IN = sum(PROJ_SPLITS)

kernel_name = 'nsa_s5_gated_hybrid_step'


def rmsnorm(x, g):
    xf = x.astype(jnp.float32)
    xf = xf * lax.rsqrt(jnp.mean(xf * xf, axis=-1, keepdims=True) + RMS_EPS)
    return xf.astype(x.dtype) * g


def masked_softmax(s, mask, axis):
    s = jnp.where(mask, s.astype(jnp.float32), -jnp.inf)
    m = jnp.max(s, axis=axis, keepdims=True)
    m = jnp.where(jnp.isfinite(m), m, 0.0)
    e = jnp.exp(s - m)
    return e / jnp.maximum(jnp.sum(e, axis=axis, keepdims=True), 1e-30)


def split_proj(p):
    B, L = p.shape[:2]
    idx = np.cumsum(PROJ_SPLITS)[:-1].tolist()
    q, kv, g_nsa, z_attn, u, z_ssm, g_merge = jnp.split(p, idx, axis=-1)
    q = q.reshape(B, L, N_KV_HEADS, GROUP, HEAD_DIM)
    kv = kv.reshape(B, L, N_KV_SLOTS, N_KV_HEADS, HEAD_DIM)
    g_nsa = jax.nn.sigmoid(g_nsa).reshape(B, L, N_KV_HEADS, GROUP, 3)
    g_merge = jax.nn.sigmoid(g_merge).reshape(B, L, 2, D_MODEL)
    return q, kv, g_nsa, z_attn, u, z_ssm, g_merge


def compress(x, pe, w1, b1, w2):
    B, T = x.shape[:2]
    n_ch = -(-T // CMP_STRIDE)
    x = jnp.pad(x, ((0, 0), (0, n_ch * CMP_STRIDE - T), (0, 0), (0, 0)))
    ch = x.reshape(B, n_ch, CMP_STRIDE, N_KV_HEADS, HEAD_DIM)
    pe = pe.reshape(CMP_RATIO, CMP_STRIDE, 1, HEAD_DIM)
    w1 = w1.reshape(CMP_RATIO, CMP_STRIDE, HEAD_DIM, HEAD_DIM)
    n_cmp = n_ch - CMP_RATIO + 1
    hid = b1
    for r in range(CMP_RATIO):
        hid = hid + jnp.einsum('bnjhd,jde->bnhe', ch[:, r:r + n_cmp] + pe[r], w1[r])
    return jnp.einsum('bnhe,ed->bnhd', jax.nn.gelu(hid), w2)


def cmp_to_sel(n_cmp, n_sel):
    c0 = jnp.arange(n_cmp)[:, None] * CMP_STRIDE
    s0 = jnp.arange(n_sel)[None, :] * SEL_BLOCK
    shared = jnp.minimum(c0 + CMP_BLOCK, s0 + SEL_BLOCK) - jnp.maximum(c0, s0)
    return jnp.clip(shared, 0, None).astype(jnp.float32) / CMP_BLOCK


def nsa_attention(q, gates, full, win, q_pos0, cmp_pe, cmp_w1, cmp_b1, cmp_w2):
    B, L = q.shape[:2]
    T = full.shape[1]
    kc = compress(full[:, :, 0], cmp_pe[0], cmp_w1[0], cmp_b1[0], cmp_w2[0])
    vc = compress(full[:, :, 1], cmp_pe[1], cmp_w1[1], cmp_b1[1], cmp_w2[1])
    n_cmp = kc.shape[1]
    cmp_end = jnp.arange(n_cmp) * CMP_STRIDE + (CMP_BLOCK - 1)
    n_sel = -(-T // SEL_BLOCK)
    sel = jnp.pad(full[:, :, 2:4], ((0, 0), (0, n_sel * SEL_BLOCK - T), (0, 0), (0, 0), (0, 0)))
    sel = sel.reshape(B, n_sel, SEL_BLOCK, 2, N_KV_HEADS, HEAD_DIM)
    ks_blk = jnp.transpose(sel[:, :, :, 0], (0, 3, 1, 2, 4))
    vs_blk = jnp.transpose(sel[:, :, :, 1], (0, 3, 1, 2, 4))
    cmp_sel = cmp_to_sel(n_cmp, n_sel)
    k_top = min(N_SEL, n_sel)
    blk = jnp.arange(n_sel)
    bi = jnp.arange(B)[:, None, None, None]
    hi = jnp.arange(N_KV_HEADS)[None, None, :, None]
    qb = math.gcd(L, Q_BLOCK)
    scale = HEAD_DIM ** -0.5

    def one_block(i):
        q0 = i * qb
        qi = lax.dynamic_slice_in_dim(q, q0, qb, axis=1) * scale
        gi = lax.dynamic_slice_in_dim(gates, q0, qb, axis=1)
        qpos = q_pos0 + q0 + jnp.arange(qb)
        s = jnp.einsum('bqhgd,bnhd->bqhgn', qi, kc)
        p_c = masked_softmax(s, (cmp_end[None, :] <= qpos[:, None])[None, :, None, None, :], -1)
        o_c = jnp.einsum('bqhgn,bnhd->bqhgd', p_c.astype(vc.dtype), vc)
        imp = jnp.einsum('bqhgn,nj->bqhj', p_c, cmp_sel)
        forced = (blk[None, :] == 0) | (blk[None, :] == (qpos // SEL_BLOCK)[:, None])
        causal = blk[None, :] * SEL_BLOCK <= qpos[:, None]
        imp = jnp.where(forced[None, :, None, :], jnp.inf,
                        jnp.where(causal[None, :, None, :], imp, -jnp.inf))
        top_v, top_i = lax.top_k(imp, k_top)
        kg = ks_blk[bi, hi, top_i]
        vg = vs_blk[bi, hi, top_i]
        kpos = top_i[..., None] * SEL_BLOCK + jnp.arange(SEL_BLOCK)
        mask_s = (top_v > -jnp.inf)[..., None] & (kpos <= qpos[None, :, None, None, None])
        s = jnp.einsum('bqhgd,bqhksd->bqhgks', qi, kg)
        p_s = masked_softmax(s, mask_s[:, :, :, None], (-2, -1))
        o_s = jnp.einsum('bqhgks,bqhksd->bqhgd', p_s.astype(vg.dtype), vg)
        wi = lax.dynamic_slice_in_dim(win, q0, WINDOW + qb, axis=1)
        kpos_w = q_pos0 - WINDOW + q0 + jnp.arange(WINDOW + qb)
        dist = qpos[:, None] - kpos_w[None, :]
        mask_w = (kpos_w[None, :] >= 0) & (dist >= 0) & (dist < WINDOW)
        s = jnp.einsum('bqhgd,bkhd->bqhgk', qi, wi[:, :, 0])
        p_w = masked_softmax(s, mask_w[None, :, None, None, :], -1)
        o_w = jnp.einsum('bqhgk,bkhd->bqhgd', p_w.astype(wi.dtype), wi[:, :, 1])
        return gi[..., 0:1] * o_c + gi[..., 1:2] * o_s + gi[..., 2:3] * o_w

    out = lax.map(one_block, jnp.arange(L // qb))
    return jnp.moveaxis(out, 0, 1).reshape(B, L, D_ATTN)


def s5_scan(u, h_re, h_im, lam_re, lam_im, log_dt, b_re, b_im, c_re, c_im):
    f32 = jnp.float32
    u, h_re, h_im = u.astype(f32), h_re.astype(f32), h_im.astype(f32)
    lam_re, lam_im = lam_re.astype(f32), lam_im.astype(f32)
    b_re, b_im, c_re, c_im = b_re.astype(f32), b_im.astype(f32), c_re.astype(f32), c_im.astype(f32)
    dt = jnp.exp(log_dt.astype(f32))[:, None]
    mag = jnp.exp(lam_re * dt)
    a_re = mag * jnp.cos(lam_im * dt)
    a_im = mag * jnp.sin(lam_im * dt)
    den = lam_re * lam_re + lam_im * lam_im
    f_re = ((a_re - 1.0) * lam_re + a_im * lam_im) / den
    f_im = (a_im * lam_re - (a_re - 1.0) * lam_im) / den
    bb_re = f_re[..., None] * b_re - f_im[..., None] * b_im
    bb_im = f_re[..., None] * b_im + f_im[..., None] * b_re
    B, L = u.shape[:2]
    tc = math.gcd(L, SSM_CHUNK)
    uc = jnp.swapaxes(u.reshape(B, L // tc, tc, N_SSM_GROUPS, SSM_GROUP), 0, 1)

    def combine(e1, e2):
        a1r, a1i, b1r, b1i = e1
        a2r, a2i, b2r, b2i = e2
        return (a2r * a1r - a2i * a1i, a2r * a1i + a2i * a1r,
                a2r * b1r - a2i * b1i + b2r, a2r * b1i + a2i * b1r + b2i)

    def step(carry, u_t):
        hr, hi = carry
        bu_re = jnp.einsum('btgc,gpc->btgp', u_t, bb_re)
        bu_im = jnp.einsum('btgc,gpc->btgp', u_t, bb_im)
        ar = jnp.broadcast_to(a_re, bu_re.shape)
        ai = jnp.broadcast_to(a_im, bu_re.shape)
        cr, ci, sr, si = lax.associative_scan(combine, (ar, ai, bu_re, bu_im), axis=1)
        hr_t = sr + cr * hr[:, None] - ci * hi[:, None]
        hi_t = si + cr * hi[:, None] + ci * hr[:, None]
        y = jnp.einsum('btgp,gcp->btgc', hr_t, c_re) - jnp.einsum('btgp,gcp->btgc', hi_t, c_im)
        return (hr_t[:, -1], hi_t[:, -1]), y

    (h_re, h_im), ys = lax.scan(step, (h_re, h_im), uc)
    return jnp.swapaxes(ys, 0, 1).reshape(B, L, N_SSM_GROUPS, SSM_GROUP), h_re, h_im


def mixer(q, full, win, g_nsa, z_attn, u, z_ssm, g_merge, q_pos0, h_re, h_im, lw):
    (cmp_pe, cmp_w1, cmp_b1, cmp_w2, lam_re, lam_im, log_dt, b_re, b_im, c_re, c_im,
     d_skip, w_glu, b_glu, w_lift_attn, w_lift_ssm, w_out) = lw
    B, L = u.shape[:2]
    o_attn = nsa_attention(q, g_nsa, full, win, q_pos0, cmp_pe, cmp_w1, cmp_b1, cmp_w2)
    branch_a = (o_attn * jax.nn.silu(z_attn)) @ w_lift_attn
    y, h_re, h_im = s5_scan(u.reshape(B, L, N_SSM_GROUPS, SSM_GROUP), h_re, h_im,
                            lam_re, lam_im, log_dt, b_re, b_im, c_re, c_im)
    y = y.reshape(B, L, D_SSM).astype(u.dtype) + d_skip * u
    y = jax.nn.gelu(y)
    y = y * jax.nn.sigmoid(y @ w_glu + b_glu)
    branch_b = (y * jax.nn.silu(z_ssm)) @ w_lift_ssm
    merged = g_merge[:, :, 0] * branch_a + g_merge[:, :, 1] * branch_b
    return merged @ w_out, h_re, h_im


def setup_inputs(seed: int = 0) -> dict:
    key = jax.random.key(seed)
    ks = jax.random.split(key, 32)
    f32 = jnp.float32
    n_pages = PAST_LEN // PAGE_SIZE
    n_used = DEC_BATCH * n_pages
    n_phys = n_used + max(1, n_used // 4)
    w_buf = min(WINDOW, PAST_LEN)

    def nrm(k, shape, s):
        return s * jax.random.normal(k, shape, f32)

    page_table = jax.random.permutation(ks[6], n_phys)[:n_used].reshape(DEC_BATCH, n_pages).astype(jnp.int32)
    lam_im = math.pi * jnp.arange(SSM_STATE, dtype=f32) + nrm(ks[13], (DEPTH, N_SSM_GROUPS, SSM_STATE), 0.01)
    return {
        'x_prompt': nrm(ks[0], (BATCH, SEQ, D_MODEL), 1.0),
        'x_sample': nrm(ks[1], (DEC_BATCH, DEC_SEQ, D_MODEL), 1.0),
        'cache_kv': nrm(ks[2], (DEPTH, n_phys, PAGE_SIZE, 4, N_KV_HEADS, HEAD_DIM), 1.0),
        'cache_win_kv': nrm(ks[3], (DEPTH, DEC_BATCH, w_buf, 2, N_KV_HEADS, HEAD_DIM), 1.0),
        'state_ssm_re': nrm(ks[4], (DEPTH, DEC_BATCH, N_SSM_GROUPS, SSM_STATE), 0.1),
        'state_ssm_im': nrm(ks[5], (DEPTH, DEC_BATCH, N_SSM_GROUPS, SSM_STATE), 0.1),
        'page_table': page_table,
        'norm_g': 1.0 + nrm(ks[7], (DEPTH, D_MODEL), 0.01),
        'w_in': nrm(ks[8], (DEPTH, D_MODEL, D_IN), D_MODEL ** -0.5),
        'cmp_pe': nrm(ks[9], (DEPTH, 2, CMP_BLOCK, HEAD_DIM), 0.02),
        'cmp_w1': nrm(ks[10], (DEPTH, 2, CMP_BLOCK, HEAD_DIM, HEAD_DIM), (CMP_BLOCK * HEAD_DIM) ** -0.5),
        'cmp_b1': nrm(ks[11], (DEPTH, 2, HEAD_DIM), 0.01),
        'cmp_w2': nrm(ks[12], (DEPTH, 2, HEAD_DIM, HEAD_DIM), HEAD_DIM ** -0.5),
        'ssm_lam_re': -0.5 + nrm(ks[14], (DEPTH, N_SSM_GROUPS, SSM_STATE), 0.01),
        'ssm_lam_im': lam_im,
        'ssm_log_dt': jax.random.uniform(ks[15], (DEPTH, N_SSM_GROUPS), f32, math.log(DT_MIN), math.log(DT_MAX)),
        'ssm_b_re': nrm(ks[16], (DEPTH, N_SSM_GROUPS, SSM_STATE, SSM_GROUP), (2 * SSM_GROUP) ** -0.5),
        'ssm_b_im': nrm(ks[17], (DEPTH, N_SSM_GROUPS, SSM_STATE, SSM_GROUP), (2 * SSM_GROUP) ** -0.5),
        'ssm_c_re': nrm(ks[18], (DEPTH, N_SSM_GROUPS, SSM_GROUP, SSM_STATE), SSM_STATE ** -0.5),
        'ssm_c_im': nrm(ks[19], (DEPTH, N_SSM_GROUPS, SSM_GROUP, SSM_STATE), SSM_STATE ** -0.5),
        'ssm_d': nrm(ks[20], (DEPTH, D_SSM), 1.0),
        'w_glu': nrm(ks[21], (DEPTH, D_SSM, D_SSM), D_SSM ** -0.5),
        'b_glu': nrm(ks[22], (DEPTH, D_SSM), 0.01),
        'w_lift_attn': nrm(ks[23], (DEPTH, D_ATTN, D_MODEL), D_ATTN ** -0.5),
        'w_lift_ssm': nrm(ks[24], (DEPTH, D_SSM, D_MODEL), D_SSM ** -0.5),
        'w_out': nrm(ks[25], (DEPTH, D_MODEL, D_MODEL), D_MODEL ** -0.5),
        'final_g': 1.0 + nrm(ks[26], (D_MODEL,), 0.01),
    }


def reference(x_prompt, x_sample, cache_kv, cache_win_kv, state_ssm_re, state_ssm_im, page_table,
              norm_g, w_in, cmp_pe, cmp_w1, cmp_b1, cmp_w2, ssm_lam_re, ssm_lam_im, ssm_log_dt,
              ssm_b_re, ssm_b_im, ssm_c_re, ssm_c_im, ssm_d, w_glu, b_glu, w_lift_attn, w_lift_ssm,
              w_out, final_g):
    past_len = page_table.shape[1] * cache_kv.shape[2]
    w_buf = cache_win_kv.shape[2]
    xp, xs = x_prompt, x_sample
    Bp, Lp = xp.shape[:2]
    Bs, Ls = xs.shape[:2]
    kvp, winp, srp, sip = [], [], [], []
    kvs, wins, srs, sis = [], [], [], []
    for l in range(DEPTH):
        lw = (cmp_pe[l], cmp_w1[l], cmp_b1[l], cmp_w2[l], ssm_lam_re[l], ssm_lam_im[l], ssm_log_dt[l],
              ssm_b_re[l], ssm_b_im[l], ssm_c_re[l], ssm_c_im[l], ssm_d[l], w_glu[l], b_glu[l],
              w_lift_attn[l], w_lift_ssm[l], w_out[l])
        q, kv, g_nsa, z_a, u, z_s, g_m = split_proj(rmsnorm(xp, norm_g[l]) @ w_in[l])
        win = jnp.pad(kv[:, :, 4:], ((0, 0), (WINDOW, 0), (0, 0), (0, 0), (0, 0)))
        h0 = jnp.zeros((Bp, N_SSM_GROUPS, SSM_STATE), jnp.float32)
        y, hr, hi = mixer(q, kv[:, :, :4], win, g_nsa, z_a, u, z_s, g_m, 0, h0, h0, lw)
        xp = xp + y
        kvp.append(kv[:, :, :4])
        winp.append(kv[:, Lp - min(WINDOW, Lp):, 4:])
        srp.append(hr)
        sip.append(hi)
        q, kv, g_nsa, z_a, u, z_s, g_m = split_proj(rmsnorm(xs, norm_g[l]) @ w_in[l])
        past = cache_kv[l][page_table].reshape(Bs, past_len, 4, N_KV_HEADS, HEAD_DIM)
        full = jnp.concatenate([past, kv[:, :, :4]], axis=1)
        buf = jnp.concatenate([cache_win_kv[l], kv[:, :, 4:]], axis=1)
        win = jnp.pad(buf, ((0, 0), (WINDOW - w_buf, 0), (0, 0), (0, 0), (0, 0)))
        y, hr, hi = mixer(q, full, win, g_nsa, z_a, u, z_s, g_m, past_len,
                          state_ssm_re[l], state_ssm_im[l], lw)
        xs = xs + y
        kvs.append(kv[:, :, :4])
        n_keep = min(WINDOW, w_buf + Ls)
        wins.append(buf[:, buf.shape[1] - n_keep:])
        srs.append(hr)
        sis.append(hi)
    y_prompt = rmsnorm(xp, final_g)
    y_sample = rmsnorm(xs, final_g)
    kv_prompt = jnp.stack(kvp)
    win_prompt = jnp.stack(winp)
    ssm_re_prompt = jnp.stack(srp)
    ssm_im_prompt = jnp.stack(sip)
    kv_sample = jnp.stack(kvs)
    win_sample = jnp.stack(wins)
    ssm_re_sample = jnp.stack(srs)
    ssm_im_sample = jnp.stack(sis)
    return (y_prompt, y_sample, kv_prompt, win_prompt, ssm_re_prompt, ssm_im_prompt,
            kv_sample, win_sample, ssm_re_sample, ssm_im_sample)
```

```python
import functools
import math

import numpy as np
import jax
import jax.numpy as jnp
from jax import lax
from jax.experimental import pallas as pl
from jax.experimental.pallas import tpu as pltpu

F32 = jnp.float32
BF16 = jnp.bfloat16

HEAD_DIM = 64
N_KV_HEADS = 2
GROUP = 4
N_Q_HEADS = N_KV_HEADS * GROUP
D_ATTN = N_Q_HEADS * HEAD_DIM
N_KV_SLOTS = 6
CMP_BLOCK = 32
CMP_STRIDE = 16
SEL_BLOCK = 64
N_SEL = 16
WINDOW = 512
Q_BLOCK = 128
SSM_GROUP = 16
SSM_STATE = 64
RMS_EPS = 1e-6
LANES = 128
KV_ROW = 4 * N_KV_HEADS * HEAD_DIM
NEG = -1e30
VMEM_LIMIT = 56 * 1024 * 1024


def _cparams(sem):
    return pltpu.CompilerParams(dimension_semantics=sem, vmem_limit_bytes=VMEM_LIMIT)


def _full(shape):
    nd = len(shape)
    return pl.BlockSpec(shape, lambda *_: (0,) * nd)


def _nt(a, b):
    return lax.dot_general(a, b, (((1,), (1,)), ((), ())), preferred_element_type=F32)


def _dot(a, b):
    return jnp.dot(a, b, preferred_element_type=F32)


def _pre_kernel(x_ref, g_ref, wq_ref, wkv_ref, wg_ref, wu_ref,
                q_ref, kv_ref, win_ref, selk_ref, selv_ref, wink_ref, winv_ref, gn_ref, u_ref,
                *, seq_len, tm):
    x = x_ref[...]
    xn = x * lax.rsqrt(jnp.mean(x * x, axis=-1, keepdims=True) + RMS_EPS) * g_ref[...]
    xb = xn.astype(BF16)
    q = _dot(xb, wq_ref[...])
    q_ref[...] = (q * (HEAD_DIM ** -0.5)).astype(BF16)
    kv = _dot(xb, wkv_ref[...])
    kv_ref[...] = kv[:, :KV_ROW]
    win_ref[...] = kv[:, KV_ROW:]
    pos = (pl.program_id(0) * tm + lax.broadcasted_iota(jnp.int32, (tm, LANES), 0)) % seq_len
    onehot = lax.broadcasted_iota(jnp.int32, (tm, LANES), 1) == (pos // SEL_BLOCK) % LANES
    selk_ref[:, :LANES] = kv[:, 256:384].astype(BF16)
    selk_ref[:, LANES:] = jnp.where(onehot, 1.0, 0.0).astype(BF16)
    selv_ref[...] = kv[:, 384:512].astype(BF16)
    wink_ref[...] = kv[:, 512:640].astype(BF16)
    winv_ref[...] = kv[:, 640:768].astype(BF16)
    gn_ref[...] = jax.nn.sigmoid(_dot(xb, wg_ref[...]))
    u_ref[...] = _dot(xb, wu_ref[...])


def _pre(x2d, norm_g, wq, wkv, wg, wu, seq_len):
    t, d = x2d.shape
    tm = min(256, t)
    kern = functools.partial(_pre_kernel, seq_len=seq_len, tm=tm)
    row = lambda n: pl.BlockSpec((tm, n), lambda i: (i, 0))
    outs = [(D_ATTN, BF16), (KV_ROW, F32), (256, F32), (256, BF16), (LANES, BF16), (LANES, BF16),
            (LANES, BF16), (LANES, F32), (wu.shape[1], F32)]
    return pl.pallas_call(
        kern,
        grid=(t // tm,),
        in_specs=[row(d), _full((1, d)), _full(wq.shape), _full(wkv.shape), _full(wg.shape), _full(wu.shape)],
        out_specs=[row(n) for n, _ in outs],
        out_shape=[jax.ShapeDtypeStruct((t, n), dt) for n, dt in outs],
        compiler_params=_cparams(("parallel",)),
        name="pre",
    )(x2d, norm_g.reshape(1, d), wq, wkv, wg, wu)


PAGES_PER_STEP = 32


def _compress_kernel(pt_ref, pages_ref, w1_ref, b_ref, w2_ref, out_ref, buf_ref, sem_ref, h_ref,
                     *, n_groups, gp):
    b = pl.program_id(0)
    j = pl.program_id(1)
    rows = gp * (128 // CMP_STRIDE)

    def copies(jj, slot):
        return [pltpu.make_async_copy(pages_ref.at[pt_ref[b, jj * gp + p], :, pl.ds(half * LANES, LANES)],
                                      buf_ref.at[slot, half, p], sem_ref.at[slot])
                for p in range(gp) for half in range(2)]

    @pl.when(j == 0)
    def _():
        for c in copies(0, 0):
            c.start()

    slot = j % 2

    @pl.when(j + 1 < n_groups)
    def _():
        for c in copies(j + 1, 1 - slot):
            c.start()

    for c in copies(j, slot):
        c.wait()

    acc = jnp.zeros((rows, 512), F32)
    for jj in range(CMP_STRIDE):
        xj = [buf_ref[slot, half, :, pl.ds(jj, 128 // CMP_STRIDE, stride=CMP_STRIDE), :].reshape(rows, LANES)
              for half in range(2)]
        xj = jnp.concatenate(xj, axis=1).astype(BF16)
        acc = acc + _dot(xj, w1_ref[jj])
    h_ref[pl.ds(pl.multiple_of(j * rows, rows), rows), :] = acc

    @pl.when(j == n_groups - 1)
    def _():
        n = h_ref.shape[0]
        hb = pltpu.roll(h_ref[:, 256:512], n - 1, 0)
        hid = h_ref[:, 0:256] + hb + b_ref[...]
        act = jax.nn.gelu(hid).astype(BF16)
        out_ref[0] = _dot(act, w2_ref[...]).astype(BF16)


def _compress(page_table, pages, w1, b_eff, w2):
    n_seq, n_pages = page_table.shape
    gp = min(PAGES_PER_STEP, n_pages)
    n_groups = n_pages // gp
    n_chunks = n_pages * (128 // CMP_STRIDE)
    kern = functools.partial(_compress_kernel, n_groups=n_groups, gp=gp)
    return pl.pallas_call(
        kern,
        grid_spec=pltpu.PrefetchScalarGridSpec(
            num_scalar_prefetch=1,
            grid=(n_seq, n_groups),
            in_specs=[pl.BlockSpec(memory_space=pl.ANY),
                      pl.BlockSpec(w1.shape, lambda b, j, pt: (0, 0, 0)),
                      pl.BlockSpec(b_eff.shape, lambda b, j, pt: (0, 0)),
                      pl.BlockSpec(w2.shape, lambda b, j, pt: (0, 0))],
            out_specs=pl.BlockSpec((1, n_chunks, 256), lambda b, j, pt: (b, 0, 0)),
            scratch_shapes=[pltpu.VMEM((2, 2, gp, 128, LANES), F32),
                            pltpu.SemaphoreType.DMA((2,)),
                            pltpu.VMEM((n_chunks, 512), F32)]),
        out_shape=jax.ShapeDtypeStruct((n_seq, n_chunks, 256), BF16),
        compiler_params=_cparams(("arbitrary", "arbitrary")),
        name="compress",
    )(page_table, pages, w1, b_eff, w2)


def _compress_weights(cmp_pe, cmp_w1, cmp_b1, cmp_w2):
    w1 = cmp_w1.reshape(2, 2, CMP_STRIDE, HEAD_DIM, HEAD_DIM)
    eye = jnp.eye(N_KV_HEADS, dtype=F32)
    blk = jnp.einsum('vrjde,vw,hg->rjvhdwge', w1, jnp.eye(2, dtype=F32), eye)
    blk = blk.reshape(2, CMP_STRIDE, 256, 256)
    w1_all = jnp.concatenate([blk[0], blk[1]], axis=-1).astype(BF16)
    pe = cmp_pe.reshape(2, 2, CMP_STRIDE, HEAD_DIM)
    pe_bias = jnp.einsum('vrjd,vrjde->ve', pe, w1) + cmp_b1
    b_eff = jnp.broadcast_to(pe_bias[:, None, :], (2, N_KV_HEADS, HEAD_DIM)).reshape(1, 256)
    w2 = jnp.einsum('ved,vw,hg->vhewgd', cmp_w2, jnp.eye(2, dtype=F32), eye).reshape(256, 256).astype(BF16)
    return w1_all, b_eff, w2


def _cmp_to_sel(n_cmp_rows, n_valid, n_sel_pad):
    c0 = np.arange(n_cmp_rows)[:, None] * CMP_STRIDE
    s0 = np.arange(n_sel_pad)[None, :] * SEL_BLOCK
    shared = np.minimum(c0 + CMP_BLOCK, s0 + SEL_BLOCK) - np.maximum(c0, s0)
    m = np.clip(shared, 0, None).astype(np.float32) / CMP_BLOCK
    m[n_valid:] = 0.0
    return m


def _masked_softmax_rows(s, mask):
    s = jnp.where(mask, s, -jnp.inf)
    m = jnp.max(s, axis=-1, keepdims=True)
    m = jnp.where(jnp.isfinite(m), m, 0.0)
    e = jnp.exp(s - m)
    return e / jnp.maximum(jnp.sum(e, axis=-1, keepdims=True), 1e-30)


def _split3_dot(p, w):
    hi = p.astype(BF16)
    r1 = p - hi.astype(F32)
    mid = r1.astype(BF16)
    lo = (r1 - mid.astype(F32)).astype(BF16)
    return _dot(hi, w) + _dot(mid, w) + _dot(lo, w)


def _topk_mask(imp, forced, causal, blk):
    v = jnp.where(forced, jnp.inf, jnp.where(causal, imp, -jnp.inf))
    big = jnp.int32(imp.shape[-1])
    sel = jnp.zeros(imp.shape, F32)
    for _ in range(N_SEL):
        mx = jnp.max(v, axis=-1, keepdims=True)
        idx = jnp.min(jnp.where(v == mx, blk, big), axis=-1, keepdims=True)
        pick = blk == idx
        sel = jnp.where(jnp.logical_and(pick, mx > -jnp.inf), 1.0, sel)
        v = jnp.where(pick, -jnp.inf, v)
    return sel > 0.0


def _online_update(m_ref, l_ref, acc_ref, s, v, rows):
    m_prev = m_ref[rows, :]
    m_cur = jnp.max(s, axis=-1, keepdims=True)
    m_new = jnp.maximum(m_prev, m_cur)
    alpha = jnp.exp(m_prev - m_new)
    p = jnp.exp(s - m_new[:, 0:1])
    l_ref[rows, :] = alpha * l_ref[rows, :] + jnp.sum(p, axis=-1, keepdims=True)
    acc_ref[rows, :] = alpha * acc_ref[rows, :] + _dot(p.astype(BF16), v)
    m_ref[rows, :] = m_new


SEL_TK = 256
ROW_CHUNK = 512


def _attn_prompt_kernel(q_ref, gn_ref, selk_ref, selv_ref, wink_ref, winv_ref, kcvc_ref, csel_ref, pq_ref,
                        o_ref, q2_ref, m_ref, l_ref, acc_ref, mw_ref, lw_ref, accw_ref):
    i = pl.program_id(1)
    nrow = N_Q_HEADS * Q_BLOCK
    qall = _dot(q_ref[...], pq_ref[...]).astype(BF16)
    q1 = jnp.concatenate([qall[:, c * LANES:(c + 1) * LANES] for c in range(N_Q_HEADS)], axis=0)
    tok = lax.broadcasted_iota(jnp.int32, (nrow, 1), 0) % Q_BLOCK
    qpos = i * Q_BLOCK + tok

    kc = kcvc_ref[0, :, 0:LANES]
    vc = kcvc_ref[0, :, LANES:2 * LANES]
    n_c = kc.shape[0]
    cmp_end = lax.broadcasted_iota(jnp.int32, (1, n_c), 1) * CMP_STRIDE + (CMP_BLOCK - 1)
    p_c = _masked_softmax_rows(_nt(q1, kc), cmp_end <= qpos)
    o_c = _dot(p_c.astype(BF16), vc)

    hrows = GROUP * Q_BLOCK
    psum = jnp.concatenate(
        [sum(p_c[h * hrows + g * Q_BLOCK:h * hrows + (g + 1) * Q_BLOCK] for g in range(GROUP))
         for h in range(N_KV_HEADS)], axis=0)
    imp = _split3_dot(psum, csel_ref[...])
    blk = lax.broadcasted_iota(jnp.int32, (1, LANES), 1)
    qpos2 = i * Q_BLOCK + lax.broadcasted_iota(jnp.int32, (N_KV_HEADS * Q_BLOCK, 1), 0) % Q_BLOCK
    forced = jnp.logical_or(blk == 0, blk == qpos2 // SEL_BLOCK)
    sel = _topk_mask(imp, forced, blk * SEL_BLOCK <= qpos2, blk)
    bias = jnp.where(sel, 0.0, NEG).astype(BF16)
    q2_ref[:, 0:LANES] = q1
    for h in range(N_KV_HEADS):
        for g in range(GROUP):
            r0 = (h * GROUP + g) * Q_BLOCK
            q2_ref[r0:r0 + Q_BLOCK, LANES:2 * LANES] = bias[h * Q_BLOCK:(h + 1) * Q_BLOCK]

    m_ref[...] = jnp.full(m_ref.shape, -jnp.inf, F32)
    l_ref[...] = jnp.zeros(l_ref.shape, F32)
    acc_ref[...] = jnp.zeros(acc_ref.shape, F32)

    def sel_step(t, causal_mask):
        k0 = pl.multiple_of(t * SEL_TK, SEL_TK)
        k2 = selk_ref[pl.ds(k0, SEL_TK), :]
        v2 = selv_ref[pl.ds(k0, SEL_TK), :]
        for r in range(nrow // ROW_CHUNK):
            rows = slice(r * ROW_CHUNK, (r + 1) * ROW_CHUNK)
            s = _nt(q2_ref[rows, :], k2)
            if causal_mask:
                kpos = k0 + lax.broadcasted_iota(jnp.int32, (1, SEL_TK), 1)
                s = jnp.where(kpos <= qpos[rows], s, NEG)
            _online_update(m_ref, l_ref, acc_ref, s, v2, rows)

    n_below = (i * Q_BLOCK) // SEL_TK

    def body(t, carry):
        sel_step(t, False)
        return carry

    lax.fori_loop(0, n_below, body, 0)
    sel_step(n_below, True)

    mw_ref[...] = jnp.full(mw_ref.shape, NEG, F32)
    lw_ref[...] = jnp.zeros(lw_ref.shape, F32)
    accw_ref[...] = jnp.zeros(accw_ref.shape, F32)
    for kt in range(WINDOW // Q_BLOCK + 1):
        t = i - WINDOW // Q_BLOCK + kt

        @pl.when(t >= 0)
        def _():
            k0 = pl.multiple_of(t * Q_BLOCK, Q_BLOCK)
            kw = wink_ref[pl.ds(k0, Q_BLOCK), :]
            vw = winv_ref[pl.ds(k0, Q_BLOCK), :]
            kpos = k0 + lax.broadcasted_iota(jnp.int32, (1, Q_BLOCK), 1)
            for r in range(nrow // ROW_CHUNK):
                rows = slice(r * ROW_CHUNK, (r + 1) * ROW_CHUNK)
                dist = qpos[rows] - kpos
                s = jnp.where(jnp.logical_and(dist >= 0, dist < WINDOW), _nt(q1[rows], kw), NEG)
                _online_update(mw_ref, lw_ref, accw_ref, s, vw, rows)

    gn = gn_ref[...]
    for c in range(N_Q_HEADS):
        h = c // GROUP
        rows = slice(c * Q_BLOCK, (c + 1) * Q_BLOCK)
        o_s = acc_ref[rows, :] / jnp.maximum(l_ref[rows, :], 1e-30)
        o_w = accw_ref[rows, :] / jnp.maximum(lw_ref[rows, :], 1e-30)
        o = gn[:, 3 * c:3 * c + 1] * o_c[rows] + gn[:, 3 * c + 1:3 * c + 2] * o_s + gn[:, 3 * c + 2:3 * c + 3] * o_w
        o_ref[:, c * HEAD_DIM:(c + 1) * HEAD_DIM] = o[:, h * HEAD_DIM:(h + 1) * HEAD_DIM]


def _q_placement():
    p = np.zeros((D_ATTN, N_Q_HEADS * LANES), np.float32)
    for c in range(N_Q_HEADS):
        h = c // GROUP
        for d in range(HEAD_DIM):
            p[c * HEAD_DIM + d, c * LANES + h * HEAD_DIM + d] = 1.0
    return jnp.asarray(p, BF16)


def _attn_prompt(q, gn, selk, selv, wink, winv, kcvc, batch, seq_len):
    nq = seq_len // Q_BLOCK
    n_c = kcvc.shape[1]
    assert seq_len % SEL_TK == 0 and seq_len // SEL_BLOCK <= LANES
    csel = jnp.asarray(_cmp_to_sel(n_c, n_c - 1, LANES), BF16)
    nrow = N_Q_HEADS * Q_BLOCK
    tile = lambda n: pl.BlockSpec((Q_BLOCK, n), lambda b, i: (b * nq + i, 0))
    seq = lambda n: pl.BlockSpec((seq_len, n), lambda b, i: (b, 0))
    return pl.pallas_call(
        _attn_prompt_kernel,
        grid=(batch, nq),
        in_specs=[tile(D_ATTN), tile(LANES), seq(2 * LANES), seq(LANES), seq(LANES), seq(LANES),
                  pl.BlockSpec((1, n_c, 256), lambda b, i: (b, 0, 0)),
                  _full(csel.shape), _full((D_ATTN, N_Q_HEADS * LANES))],
        out_specs=tile(D_ATTN),
        out_shape=jax.ShapeDtypeStruct((batch * seq_len, D_ATTN), F32),
        scratch_shapes=[pltpu.VMEM((nrow, 2 * LANES), BF16)] + [pltpu.VMEM((nrow, LANES), F32)] * 6,
        compiler_params=_cparams(("parallel", "arbitrary")),
        name="attn_prompt",
    )(q, gn, selk, selv, wink, winv, kcvc, csel, _q_placement())


SAMPLE_PAGES_PER_STEP = 32


def _attn_sample_kernel(pt_ref, q_ref, gn_ref, kcvc_ref, csel_ref, pq_ref, oh_ref, selk_new_ref, selv_new_ref,
                        wink_ref, winv_ref, pages_ref, o_ref,
                        buf_ref, sem_ref, q1_ref, bias_ref, oc_ref, m_ref, l_ref, acc_ref, ow_ref,
                        *, n_groups, gp, past_len, ls):
    b = pl.program_id(0)
    j = pl.program_id(1)
    nrow = N_Q_HEADS * ls
    tk = gp * 128
    blocks_per_step = tk // SEL_BLOCK

    def copies(jj, slot):
        return [pltpu.make_async_copy(pages_ref.at[pt_ref[b, jj * gp + p], :, pl.ds(256, 256)],
                                      buf_ref.at[slot, p], sem_ref.at[slot])
                for p in range(gp)]

    @pl.when(j == 0)
    def _():
        for c in copies(0, 0):
            c.start()

    slot = j % 2

    @pl.when(j + 1 < n_groups)
    def _():
        for c in copies(j + 1, 1 - slot):
            c.start()

    tok = lax.broadcasted_iota(jnp.int32, (nrow, 1), 0) % ls
    qpos = past_len + tok

    @pl.when(j == 0)
    def _():
        qall = _dot(q_ref[0], pq_ref[...]).astype(BF16)
        q1 = jnp.concatenate([qall[:, c * LANES:(c + 1) * LANES] for c in range(N_Q_HEADS)], axis=0)
        q1_ref[...] = q1
        kc = kcvc_ref[0, :, 0:LANES]
        vc = kcvc_ref[0, :, LANES:2 * LANES]
        n_c = kc.shape[0]
        cmp_end = lax.broadcasted_iota(jnp.int32, (1, n_c), 1) * CMP_STRIDE + (CMP_BLOCK - 1)
        p_c = _masked_softmax_rows(_nt(q1, kc), cmp_end <= qpos)
        oc_ref[...] = _dot(p_c.astype(BF16), vc)
        hrows = GROUP * ls
        psum = jnp.concatenate(
            [sum(p_c[h * hrows + g * ls:h * hrows + (g + 1) * ls] for g in range(GROUP))
             for h in range(N_KV_HEADS)], axis=0)
        imp = _split3_dot(psum, csel_ref[...])
        n_sel_pad = imp.shape[1]
        blk = lax.broadcasted_iota(jnp.int32, (1, n_sel_pad), 1)
        qpos2 = past_len + lax.broadcasted_iota(jnp.int32, (N_KV_HEADS * ls, 1), 0) % ls
        forced = jnp.logical_or(blk == 0, blk == qpos2 // SEL_BLOCK)
        sel = _topk_mask(imp, forced, blk * SEL_BLOCK <= qpos2, blk)
        bias = jnp.where(sel, 0.0, NEG).astype(BF16)
        for h in range(N_KV_HEADS):
            for g in range(GROUP):
                r0 = (h * GROUP + g) * ls
                bias_ref[r0:r0 + ls, :] = bias[h * ls:(h + 1) * ls]
        kpos_new = past_len + lax.broadcasted_iota(jnp.int32, (1, ls), 1)
        s = jnp.where(kpos_new <= qpos, _nt(q1, selk_new_ref[0]), NEG)
        m0 = jnp.max(s, axis=-1, keepdims=True)
        p = jnp.exp(s - m0)
        m_ref[...] = jnp.broadcast_to(m0, m_ref.shape)
        l_ref[...] = jnp.broadcast_to(jnp.sum(p, axis=-1, keepdims=True), l_ref.shape)
        acc_ref[...] = _dot(p.astype(BF16), selv_new_ref[0])
        n_w = wink_ref.shape[1]
        kpos_w = past_len + ls - n_w + lax.broadcasted_iota(jnp.int32, (1, n_w), 1)
        dist = qpos - kpos_w
        mask_w = jnp.logical_and(jnp.logical_and(kpos_w >= 0, dist >= 0), dist < WINDOW)
        p_w = _masked_softmax_rows(_nt(q1, wink_ref[0]), mask_w)
        ow_ref[...] = _dot(p_w.astype(BF16), winv_ref[0])

    for c in copies(j, slot):
        c.wait()

    kv = buf_ref[slot].reshape(tk, 256)
    k2 = jnp.concatenate([kv[:, 0:LANES].astype(BF16), oh_ref[...]], axis=1)
    v2 = kv[:, LANES:2 * LANES].astype(BF16)
    b0 = pl.multiple_of((j * blocks_per_step // LANES) * LANES, LANES)
    q2 = jnp.concatenate([q1_ref[...], bias_ref[:, pl.ds(b0, LANES)]], axis=1)
    s = _nt(q2, k2)
    _online_update(m_ref, l_ref, acc_ref, s, v2, slice(0, nrow))

    @pl.when(j == n_groups - 1)
    def _():
        gn = gn_ref[0]
        for c in range(N_Q_HEADS):
            h = c // GROUP
            rows = slice(c * ls, (c + 1) * ls)
            o_s = acc_ref[rows, :] / jnp.maximum(l_ref[rows, :], 1e-30)
            o = (gn[:, 3 * c:3 * c + 1] * oc_ref[rows, :] + gn[:, 3 * c + 1:3 * c + 2] * o_s
                 + gn[:, 3 * c + 2:3 * c + 3] * ow_ref[rows, :])
            o_ref[0, :, c * HEAD_DIM:(c + 1) * HEAD_DIM] = o[:, h * HEAD_DIM:(h + 1) * HEAD_DIM]


def _attn_sample(page_table, q, gn, kcvc, selk_new, selv_new, wink, winv, pages, past_len, ls):
    bs, n_pages = page_table.shape
    gp = min(SAMPLE_PAGES_PER_STEP, n_pages)
    n_groups = n_pages // gp
    tk = gp * 128
    blocks_per_step = tk // SEL_BLOCK
    assert LANES % blocks_per_step == 0
    n_c = kcvc.shape[1]
    n_sel = -(-(past_len + ls) // SEL_BLOCK)
    n_sel_pad = -(-n_sel // LANES) * LANES
    csel = jnp.asarray(_cmp_to_sel(n_c, n_c - 1, n_sel_pad), BF16)
    key_blk = (np.arange(tk) // SEL_BLOCK)[:, None]
    onehot = [jnp.asarray((key_blk + s * blocks_per_step) % LANES == np.arange(LANES)[None, :], BF16)
              for s in range(LANES // blocks_per_step)]
    onehot = jnp.stack(onehot)
    n_rep = LANES // blocks_per_step
    nrow = N_Q_HEADS * ls
    kern = functools.partial(_attn_sample_kernel, n_groups=n_groups, gp=gp, past_len=past_len, ls=ls)
    per_seq = lambda *shape: pl.BlockSpec((1,) + shape, lambda b, j, pt: (b,) + (0,) * len(shape))
    const = lambda shape: pl.BlockSpec(shape, lambda b, j, pt: (0,) * len(shape))
    return pl.pallas_call(
        kern,
        grid_spec=pltpu.PrefetchScalarGridSpec(
            num_scalar_prefetch=1,
            grid=(bs, n_groups),
            in_specs=[per_seq(ls, D_ATTN), per_seq(ls, LANES), per_seq(n_c, 256),
                      const(csel.shape), const((D_ATTN, N_Q_HEADS * LANES)),
                      pl.BlockSpec((None, tk, LANES), lambda b, j, pt: (j % n_rep, 0, 0)),
                      per_seq(ls, LANES), per_seq(ls, LANES),
                      per_seq(wink.shape[1], LANES), per_seq(winv.shape[1], LANES),
                      pl.BlockSpec(memory_space=pl.ANY)],
            out_specs=per_seq(ls, D_ATTN),
            scratch_shapes=[pltpu.VMEM((2, gp, 128, 256), F32),
                            pltpu.SemaphoreType.DMA((2,)),
                            pltpu.VMEM((nrow, LANES), BF16),
                            pltpu.VMEM((nrow, n_sel_pad), BF16),
                            pltpu.VMEM((nrow, LANES), F32),
                            pltpu.VMEM((nrow, LANES), F32),
                            pltpu.VMEM((nrow, LANES), F32),
                            pltpu.VMEM((nrow, LANES), F32),
                            pltpu.VMEM((nrow, LANES), F32)]),
        out_shape=jax.ShapeDtypeStruct((bs, ls, D_ATTN), F32),
        compiler_params=_cparams(("arbitrary", "arbitrary")),
        name="attn_sample",
    )(page_table, q, gn, kcvc, csel, _q_placement(), onehot, selk_new, selv_new, wink, winv, pages)


def _cmul(ar, ai, br, bi):
    return ar * br - ai * bi, ar * bi + ai * br


def _ssm_weights(lam_re, lam_im, log_dt, b_re, b_im, c_re, c_im, tc):
    dt = jnp.exp(log_dt)[:, None]
    mag = jnp.exp(lam_re * dt)
    a_re = mag * jnp.cos(lam_im * dt)
    a_im = mag * jnp.sin(lam_im * dt)
    den = lam_re * lam_re + lam_im * lam_im
    f_re = ((a_re - 1.0) * lam_re + a_im * lam_im) / den
    f_im = (a_im * lam_re - (a_re - 1.0) * lam_im) / den
    bb_re = f_re[..., None] * b_re - f_im[..., None] * b_im
    bb_im = f_re[..., None] * b_im + f_im[..., None] * b_re
    pr, pi = jnp.ones_like(a_re)[None], jnp.zeros_like(a_re)[None]
    sr, si = a_re, a_im
    while pr.shape[0] < tc + 1:
        nr, ni = _cmul(pr, pi, sr[None], si[None])
        pr, pi = jnp.concatenate([pr, nr]), jnp.concatenate([pi, ni])
        sr, si = _cmul(sr, si, sr, si)
    pr, pi = pr[:tc + 1], pi[:tc + 1]
    cpr = c_re[None] * pr[:, :, None, :] - c_im[None] * pi[:, :, None, :]
    cpi = c_re[None] * pi[:, :, None, :] + c_im[None] * pr[:, :, None, :]
    kker = jnp.einsum('tgop,gpi->gtoi', cpr[:tc], bb_re) - jnp.einsum('tgop,gpi->gtoi', cpi[:tc], bb_im)
    lag = np.arange(tc)[None, :] - np.arange(tc)[:, None]
    kpad = jnp.concatenate([kker, jnp.zeros_like(kker[:, :1])], axis=1)
    toep = kpad[:, np.where(lag >= 0, lag, tc)]
    g = toep.shape[0]
    toep = jnp.transpose(toep, (0, 1, 4, 2, 3)).reshape(g, tc * SSM_GROUP, tc * SSM_GROUP)
    qr, qi = pr[:tc][::-1], pi[:tc][::-1]
    n_re = qr[..., None] * bb_re[None] - qi[..., None] * bb_im[None]
    n_im = qr[..., None] * bb_im[None] + qi[..., None] * bb_re[None]
    nmat = jnp.concatenate([n_re, n_im], axis=2)
    nmat = jnp.transpose(nmat, (1, 0, 3, 2)).reshape(g, tc * SSM_GROUP, 2 * SSM_STATE)
    m_re = cpr[1:tc + 1]
    m_im = -cpi[1:tc + 1]
    mmat = jnp.concatenate([m_re, m_im], axis=3)
    mmat = jnp.transpose(mmat, (1, 3, 0, 2)).reshape(g, 2 * SSM_STATE, tc * SSM_GROUP)
    return toep, nmat, mmat, pr[tc], pi[tc]


def _pair_expand_n(nmat):
    g, k, _ = nmat.shape
    out = jnp.zeros((g, k, 2, 2, SSM_STATE), nmat.dtype)
    par = np.arange(g) % 2
    n4 = nmat.reshape(g, k, 2, SSM_STATE)
    out = out.at[np.arange(g), :, :, par, :].set(n4)
    return out.reshape(g, k, 4 * SSM_STATE)


def _pair_expand_m(mmat):
    g, _, n = mmat.shape
    out = jnp.zeros((g, 2, 2, SSM_STATE, n), mmat.dtype)
    par = np.arange(g) % 2
    m4 = mmat.reshape(g, 2, SSM_STATE, n)
    out = out.at[np.arange(g), :, par, :, :].set(m4)
    return out.reshape(g, 4 * SSM_STATE, n)


def _ssm_state_kernel(u_ref, n_ref, sre_ref, sim_ref):
    s = _dot(u_ref[0], n_ref[0]) + _dot(u_ref[1], n_ref[1])
    sre_ref[...] = s[:, 0:LANES]
    sim_ref[...] = s[:, LANES:2 * LANES]


def _ssm_carry_kernel(sre_ref, sim_ref, h0re_ref, h0im_ref, are_ref, aim_ref,
                      hre_ref, him_ref, fre_ref, fim_ref):
    are = are_ref[...]
    aim = aim_ref[...]

    def body(k, carry):
        hr, hi = carry
        hre_ref[k] = hr
        him_ref[k] = hi
        nr = are * hr - aim * hi + sre_ref[k]
        ni = are * hi + aim * hr + sim_ref[k]
        return nr, ni

    hr, hi = lax.fori_loop(0, sre_ref.shape[0], body, (h0re_ref[...], h0im_ref[...]))
    fre_ref[...] = hr
    fim_ref[...] = hi


def _ssm_out_kernel(u_ref, t_ref, m_ref, hre_ref, him_ref, y_ref):
    hp = jnp.concatenate([hre_ref[...], him_ref[...]], axis=1).astype(BF16)
    for e in range(2):
        y_ref[e] = _dot(u_ref[e], t_ref[e]) + _dot(hp, m_ref[e])


def _ssm(u2d, batch, seq_len, tc, h0_re, h0_im, ssm_w):
    toep, nmat, mmat, at_re, at_im = ssm_w
    g = toep.shape[0]
    nck = seq_len // tc
    r = nck * batch
    kk = tc * SSM_GROUP
    ug = u2d.astype(BF16).reshape(batch, nck, tc, g, SSM_GROUP)
    ug = jnp.transpose(ug, (3, 1, 0, 2, 4)).reshape(g, r, kk)
    gs = g * SSM_STATE
    pair = lambda *shape: pl.BlockSpec((2,) + shape, lambda j: (j,) + (0,) * len(shape))
    col = pl.BlockSpec((r, LANES), lambda j: (0, j))
    s_re, s_im = pl.pallas_call(
        _ssm_state_kernel,
        grid=(g // 2,),
        in_specs=[pair(r, kk), pair(kk, 4 * SSM_STATE)],
        out_specs=[col, col],
        out_shape=[jax.ShapeDtypeStruct((r, gs), F32)] * 2,
        compiler_params=_cparams(("parallel",)),
        name="ssm_state",
    )(ug, _pair_expand_n(nmat).astype(BF16))
    cw = 512
    seq3 = pl.BlockSpec((nck, batch, cw), lambda c: (0, 0, c))
    vec = lambda n: pl.BlockSpec((n, cw), lambda c: (0, c))
    h_re, h_im, f_re, f_im = pl.pallas_call(
        _ssm_carry_kernel,
        grid=(gs // cw,),
        in_specs=[seq3, seq3, vec(batch), vec(batch), vec(1), vec(1)],
        out_specs=[seq3, seq3, vec(batch), vec(batch)],
        out_shape=[jax.ShapeDtypeStruct((nck, batch, gs), F32)] * 2 + [jax.ShapeDtypeStruct((batch, gs), F32)] * 2,
        compiler_params=_cparams(("parallel",)),
        name="ssm_carry",
    )(s_re.reshape(nck, batch, gs), s_im.reshape(nck, batch, gs), h0_re.reshape(batch, gs),
      h0_im.reshape(batch, gs), at_re.reshape(1, gs), at_im.reshape(1, gs))
    h_re, h_im = h_re.reshape(r, gs), h_im.reshape(r, gs)
    y = pl.pallas_call(
        _ssm_out_kernel,
        grid=(g // 2,),
        in_specs=[pair(r, kk), pair(kk, kk), pair(4 * SSM_STATE, kk), col, col],
        out_specs=pair(r, kk),
        out_shape=jax.ShapeDtypeStruct((g, r, kk), F32),
        compiler_params=_cparams(("parallel",)),
        name="ssm_out",
    )(ug, toep.astype(BF16), _pair_expand_m(mmat).astype(BF16), h_re, h_im)
    y = jnp.transpose(y.reshape(g, nck, batch, tc, SSM_GROUP), (2, 1, 3, 0, 4)).reshape(batch * seq_len, g * SSM_GROUP)
    return y, f_re.reshape(batch, g, SSM_STATE), f_im.reshape(batch, g, SSM_STATE)


def _post_kernel(x_ref, oa_ref, ys_ref, u_ref, g_ref, fg_ref, d_ref, bglu_ref,
                 wza_ref, wzs_ref, wgm_ref, wglu_ref, wla_ref, wls_ref, wo_ref, out_ref):
    x = x_ref[...]
    d = x.shape[1]
    xn = x * lax.rsqrt(jnp.mean(x * x, axis=-1, keepdims=True) + RMS_EPS) * g_ref[...]
    xb = xn.astype(BF16)
    z_a = _dot(xb, wza_ref[...])
    z_s = _dot(xb, wzs_ref[...])
    gm = jax.nn.sigmoid(_dot(xb, wgm_ref[...]))
    branch_a = _dot((oa_ref[...] * jax.nn.silu(z_a)).astype(BF16), wla_ref[...])
    y = jax.nn.gelu(ys_ref[...] + d_ref[...] * u_ref[...])
    y = y * jax.nn.sigmoid(_dot(y.astype(BF16), wglu_ref[...]) + bglu_ref[...])
    branch_b = _dot((y * jax.nn.silu(z_s)).astype(BF16), wls_ref[...])
    merged = gm[:, :d] * branch_a + gm[:, d:] * branch_b
    r = x + _dot(merged.astype(BF16), wo_ref[...])
    out_ref[...] = r * lax.rsqrt(jnp.mean(r * r, axis=-1, keepdims=True) + RMS_EPS) * fg_ref[...]


def _post(x2d, o_attn, y_ssm, u, norm_g, final_g, ssm_d, b_glu, wza, wzs, wgm, wglu, wla, wls, wo):
    t, d = x2d.shape
    tm = min(256, t)
    row = lambda n: pl.BlockSpec((tm, n), lambda i: (i, 0))
    ws = [wza, wzs, wgm, wglu, wla, wls, wo]
    vecs = [norm_g.reshape(1, d), final_g.reshape(1, d), ssm_d.reshape(1, -1), b_glu.reshape(1, -1)]
    return pl.pallas_call(
        _post_kernel,
        grid=(t // tm,),
        in_specs=[row(d), row(o_attn.shape[1]), row(y_ssm.shape[1]), row(u.shape[1])]
                 + [_full(v.shape) for v in vecs] + [_full(w.shape) for w in ws],
        out_specs=row(d),
        out_shape=jax.ShapeDtypeStruct((t, d), F32),
        compiler_params=_cparams(("parallel",)),
        name="post",
    )(x2d, o_attn, y_ssm, u, *vecs, *ws)


SSM_CHUNK_PROMPT = 64


def kernel(x_prompt, x_sample, cache_kv, cache_win_kv, state_ssm_re, state_ssm_im, page_table, norm_g, w_in, cmp_pe, cmp_w1, cmp_b1, cmp_w2, ssm_lam_re, ssm_lam_im, ssm_log_dt, ssm_b_re, ssm_b_im, ssm_c_re, ssm_c_im, ssm_d, w_glu, b_glu, w_lift_attn, w_lift_ssm, w_out, final_g):
    depth = norm_g.shape[0]
    assert depth == 1
    l = 0
    bp, lp, d = x_prompt.shape
    bs, ls, _ = x_sample.shape
    n_pages, page = page_table.shape[1], cache_kv.shape[2]
    past_len = n_pages * page
    w_buf = cache_win_kv.shape[2]
    d_ssm = ssm_d.shape[1]
    n_grp = d_ssm // SSM_GROUP
    assert page == 128 and w_buf == WINDOW and lp % Q_BLOCK == 0 and lp >= WINDOW

    splits = (D_ATTN, N_KV_SLOTS * N_KV_HEADS * HEAD_DIM, 3 * N_Q_HEADS, D_ATTN, d_ssm, d_ssm, 2 * d)
    offs = np.concatenate([[0], np.cumsum(splits)])
    wb = w_in[l].astype(BF16)
    wq, wkv, wg, wza, wu, wzs, wgm = [wb[:, offs[k]:offs[k + 1]] for k in range(7)]
    wg = jnp.pad(wg, ((0, 0), (0, LANES - wg.shape[1])))
    cw1, cb, cw2 = _compress_weights(cmp_pe[l], cmp_w1[l], cmp_b1[l], cmp_w2[l])
    ssm_p = (ssm_lam_re[l], ssm_lam_im[l], ssm_log_dt[l], ssm_b_re[l], ssm_b_im[l], ssm_c_re[l], ssm_c_im[l])
    post_w = (norm_g[l], final_g, ssm_d[l], b_glu[l], wza, wzs, wgm, w_glu[l].astype(BF16),
              w_lift_attn[l].astype(BF16), w_lift_ssm[l].astype(BF16), w_out[l].astype(BF16))

    xp = x_prompt.reshape(bp * lp, d)
    q, kv, win, selk, selv, wink, winv, gn, u = _pre(xp, norm_g[l], wq, wkv, wg, wu, lp)
    ident = jnp.arange(bp * (lp // 128), dtype=jnp.int32).reshape(bp, lp // 128)
    kcvc = _compress(ident, kv.reshape(bp * lp // 128, 128, KV_ROW), cw1, cb, cw2)
    o_attn = _attn_prompt(q, gn, selk, selv, wink, winv, kcvc, bp, lp)
    tcp = math.gcd(lp, SSM_CHUNK_PROMPT)
    h0 = jnp.zeros((bp, n_grp, SSM_STATE), F32)
    y_ssm, hr_p, hi_p = _ssm(u, bp, lp, tcp, h0, h0, _ssm_weights(*ssm_p, tcp))
    y_prompt = _post(xp, o_attn, y_ssm, u, *post_w).reshape(bp, lp, d)
    kv_prompt = kv.reshape(1, bp, lp, 4, N_KV_HEADS, HEAD_DIM)
    win_prompt = win.reshape(bp, lp, 2, N_KV_HEADS, HEAD_DIM)[None, :, lp - WINDOW:]

    xs = x_sample.reshape(bs * ls, d)
    q, kv, win, selk, selv, wink, winv, gn, u = _pre(xs, norm_g[l], wq, wkv, wg, wu, ls)
    pages = cache_kv[l].reshape(-1, 128, KV_ROW)
    kcvc = _compress(page_table, pages, cw1, cb, cw2)
    cwin = cache_win_kv[l].reshape(bs, w_buf, 2 * LANES)
    wk_all = jnp.concatenate([cwin[:, :, :LANES].astype(BF16), wink.reshape(bs, ls, LANES)], axis=1)
    wv_all = jnp.concatenate([cwin[:, :, LANES:].astype(BF16), winv.reshape(bs, ls, LANES)], axis=1)
    o_attn = _attn_sample(page_table, q.reshape(bs, ls, D_ATTN), gn.reshape(bs, ls, LANES), kcvc,
                          selk[:, :LANES].reshape(bs, ls, LANES), selv.reshape(bs, ls, LANES),
                          wk_all, wv_all, pages, past_len, ls)
    y_ssm, hr_s, hi_s = _ssm(u, bs, ls, ls, state_ssm_re[l], state_ssm_im[l], _ssm_weights(*ssm_p, ls))
    y_sample = _post(xs, o_attn.reshape(bs * ls, D_ATTN), y_ssm, u, *post_w).reshape(bs, ls, d)
    kv_sample = kv.reshape(1, bs, ls, 4, N_KV_HEADS, HEAD_DIM)
    win_new = win.reshape(bs, ls, 2, N_KV_HEADS, HEAD_DIM)
    win_sample = jnp.concatenate([cache_win_kv[l], win_new], axis=1)[None, :, ls:]

    return (y_prompt, y_sample, kv_prompt, win_prompt, hr_p[None], hi_p[None],
            kv_sample, win_sample, hr_s[None], hi_s[None])
```

```python
import functools
import math

import numpy as np
import jax
import jax.numpy as jnp
from jax import lax
from jax.experimental import pallas as pl
from jax.experimental.pallas import tpu as pltpu

F32 = jnp.float32
BF16 = jnp.bfloat16

HEAD_DIM = 64
N_KV_HEADS = 2
GROUP = 4
N_Q_HEADS = N_KV_HEADS * GROUP
D_ATTN = N_Q_HEADS * HEAD_DIM
N_KV_SLOTS = 6
CMP_BLOCK = 32
CMP_STRIDE = 16
SEL_BLOCK = 64
N_SEL = 16
WINDOW = 512
Q_BLOCK = 128
SSM_GROUP = 16
SSM_STATE = 64
RMS_EPS = 1e-6
LANES = 128
KV_ROW = 4 * N_KV_HEADS * HEAD_DIM
NEG = -1e30
LOG2E = math.log2(math.e)
VMEM_LIMIT = 56 * 1024 * 1024


def _cparams(sem):
    return pltpu.CompilerParams(dimension_semantics=sem, vmem_limit_bytes=VMEM_LIMIT)


def _full(shape):
    nd = len(shape)
    return pl.BlockSpec(shape, lambda *_: (0,) * nd)


def _nt(a, b):
    return lax.dot_general(a, b, (((1,), (1,)), ((), ())), preferred_element_type=F32)


def _dot(a, b):
    return jnp.dot(a, b, preferred_element_type=F32)


def _pre_kernel(x_ref, g_ref, wq_ref, wkv_ref, wg_ref, wu_ref,
                q_ref, kv_ref, win_ref, selk_ref, selv_ref, wink_ref, winv_ref, gn_ref, u_ref,
                *, seq_len, tm):
    x = x_ref[...]
    xn = x * lax.rsqrt(jnp.mean(x * x, axis=-1, keepdims=True) + RMS_EPS) * g_ref[...]
    xb = xn.astype(BF16)
    q = _dot(xb, wq_ref[...])
    q_ref[...] = (q * (LOG2E * HEAD_DIM ** -0.5)).astype(BF16)
    kv = _dot(xb, wkv_ref[...])
    kv_ref[...] = kv[:, :KV_ROW]
    win_ref[...] = kv[:, KV_ROW:]
    pos = (pl.program_id(0) * tm + lax.broadcasted_iota(jnp.int32, (tm, LANES), 0)) % seq_len
    onehot = lax.broadcasted_iota(jnp.int32, (tm, LANES), 1) == (pos // SEL_BLOCK) % LANES
    selk_ref[:, :LANES] = kv[:, 256:384].astype(BF16)
    selk_ref[:, LANES:] = jnp.where(onehot, 1.0, 0.0).astype(BF16)
    selv_ref[...] = kv[:, 384:512].astype(BF16)
    wink_ref[...] = kv[:, 512:640].astype(BF16)
    winv_ref[...] = kv[:, 640:768].astype(BF16)
    gn_ref[...] = jax.nn.sigmoid(_dot(xb, wg_ref[...]))
    u_ref[...] = _dot(xb, wu_ref[...])


def _pre(x2d, norm_g, wq, wkv, wg, wu, seq_len):
    t, d = x2d.shape
    tm = min(256, t)
    kern = functools.partial(_pre_kernel, seq_len=seq_len, tm=tm)
    row = lambda n: pl.BlockSpec((tm, n), lambda i: (i, 0))
    outs = [(D_ATTN, BF16), (KV_ROW, F32), (256, F32), (256, BF16), (LANES, BF16), (LANES, BF16),
            (LANES, BF16), (LANES, F32), (wu.shape[1], F32)]
    return pl.pallas_call(
        kern,
        grid=(t // tm,),
        in_specs=[row(d), _full((1, d)), _full(wq.shape), _full(wkv.shape), _full(wg.shape), _full(wu.shape)],
        out_specs=[row(n) for n, _ in outs],
        out_shape=[jax.ShapeDtypeStruct((t, n), dt) for n, dt in outs],
        compiler_params=_cparams(("parallel",)),
        name="pre",
    )(x2d, norm_g.reshape(1, d), wq, wkv, wg, wu)


PAGES_PER_STEP = 32


def _compress_kernel(pt_ref, pages_ref, w1_ref, b_ref, w2_ref, out_ref, buf_ref, sem_ref, h_ref,
                     *, n_groups, gp):
    b = pl.program_id(0)
    j = pl.program_id(1)
    rows = gp * (128 // CMP_STRIDE)

    def copies(jj, slot):
        return [pltpu.make_async_copy(pages_ref.at[pt_ref[b, jj * gp + p], :, pl.ds(half * LANES, LANES)],
                                      buf_ref.at[slot, half, p], sem_ref.at[slot])
                for p in range(gp) for half in range(2)]

    @pl.when(j == 0)
    def _():
        for c in copies(0, 0):
            c.start()

    slot = j % 2

    @pl.when(j + 1 < n_groups)
    def _():
        for c in copies(j + 1, 1 - slot):
            c.start()

    for c in copies(j, slot):
        c.wait()

    acc = jnp.zeros((rows, 512), F32)
    for jj in range(CMP_STRIDE):
        xj = [buf_ref[slot, half, :, pl.ds(jj, 128 // CMP_STRIDE, stride=CMP_STRIDE), :].reshape(rows, LANES)
              for half in range(2)]
        xj = jnp.concatenate(xj, axis=1).astype(BF16)
        acc = acc + _dot(xj, w1_ref[jj])
    h_ref[pl.ds(pl.multiple_of(j * rows, rows), rows), :] = acc

    @pl.when(j == n_groups - 1)
    def _():
        n = h_ref.shape[0]
        hb = pltpu.roll(h_ref[:, 256:512], n - 1, 0)
        hid = h_ref[:, 0:256] + hb + b_ref[...]
        act = jax.nn.gelu(hid).astype(BF16)
        out_ref[0] = _dot(act, w2_ref[...]).astype(BF16)


def _compress(page_table, pages, w1, b_eff, w2):
    n_seq, n_pages = page_table.shape
    gp = min(PAGES_PER_STEP, n_pages)
    n_groups = n_pages // gp
    n_chunks = n_pages * (128 // CMP_STRIDE)
    kern = functools.partial(_compress_kernel, n_groups=n_groups, gp=gp)
    return pl.pallas_call(
        kern,
        grid_spec=pltpu.PrefetchScalarGridSpec(
            num_scalar_prefetch=1,
            grid=(n_seq, n_groups),
            in_specs=[pl.BlockSpec(memory_space=pl.ANY),
                      pl.BlockSpec(w1.shape, lambda b, j, pt: (0, 0, 0)),
                      pl.BlockSpec(b_eff.shape, lambda b, j, pt: (0, 0)),
                      pl.BlockSpec(w2.shape, lambda b, j, pt: (0, 0))],
            out_specs=pl.BlockSpec((1, n_chunks, 256), lambda b, j, pt: (b, 0, 0)),
            scratch_shapes=[pltpu.VMEM((2, 2, gp, 128, LANES), F32),
                            pltpu.SemaphoreType.DMA((2,)),
                            pltpu.VMEM((n_chunks, 512), F32)]),
        out_shape=jax.ShapeDtypeStruct((n_seq, n_chunks, 256), BF16),
        compiler_params=_cparams(("arbitrary", "arbitrary")),
        name="compress",
    )(page_table, pages, w1, b_eff, w2)


def _compress_weights(cmp_pe, cmp_w1, cmp_b1, cmp_w2):
    w1 = cmp_w1.reshape(2, 2, CMP_STRIDE, HEAD_DIM, HEAD_DIM)
    eye = jnp.eye(N_KV_HEADS, dtype=F32)
    blk = jnp.einsum('vrjde,vw,hg->rjvhdwge', w1, jnp.eye(2, dtype=F32), eye)
    blk = blk.reshape(2, CMP_STRIDE, 256, 256)
    w1_all = jnp.concatenate([blk[0], blk[1]], axis=-1).astype(BF16)
    pe = cmp_pe.reshape(2, 2, CMP_STRIDE, HEAD_DIM)
    pe_bias = jnp.einsum('vrjd,vrjde->ve', pe, w1) + cmp_b1
    b_eff = jnp.broadcast_to(pe_bias[:, None, :], (2, N_KV_HEADS, HEAD_DIM)).reshape(1, 256)
    w2 = jnp.einsum('ved,vw,hg->vhewgd', cmp_w2, jnp.eye(2, dtype=F32), eye).reshape(256, 256).astype(BF16)
    return w1_all, b_eff, w2


def _cmp_to_sel(n_cmp_rows, n_valid, n_sel_pad):
    c0 = np.arange(n_cmp_rows)[:, None] * CMP_STRIDE
    s0 = np.arange(n_sel_pad)[None, :] * SEL_BLOCK
    shared = np.minimum(c0 + CMP_BLOCK, s0 + SEL_BLOCK) - np.maximum(c0, s0)
    m = np.clip(shared, 0, None).astype(np.float32) / CMP_BLOCK
    m[n_valid:] = 0.0
    return m


def _softmax2_rows(s, mask):
    s = jnp.where(mask, s, -jnp.inf)
    m = jnp.max(s, axis=-1, keepdims=True)
    m = jnp.where(jnp.isfinite(m), m, 0.0)
    e = jnp.exp2(s - m)
    return e / jnp.maximum(jnp.sum(e, axis=-1, keepdims=True), 1e-30)


def _split_dot(p, w):
    hi = p.astype(BF16)
    lo = (p - hi.astype(F32)).astype(BF16)
    return _dot(hi, w) + _dot(lo, w)


def _topk_mask_t(imp_t, forced_t, causal_t, blk_t):
    v = jnp.where(forced_t, jnp.inf, jnp.where(causal_t, imp_t, -jnp.inf))
    big = jnp.int32(imp_t.shape[0])
    sel = jnp.zeros(imp_t.shape, F32)
    for _ in range(N_SEL):
        mx = jnp.max(v, axis=0, keepdims=True)
        idx = jnp.min(jnp.where(v == mx, blk_t, big), axis=0, keepdims=True)
        pick = blk_t == idx
        sel = jnp.where(pick, jnp.maximum(sel, jnp.where(mx > -jnp.inf, 1.0, 0.0)), sel)
        v = jnp.where(pick, -jnp.inf, v)
    return sel


def _lane_groups(s):
    return [s[:, j * LANES:(j + 1) * LANES] for j in range(s.shape[1] // LANES)]


def _one_ahead(items, make):
    nxt = make(items[0])
    for n, item in enumerate(items):
        cur = nxt
        if n + 1 < len(items):
            nxt = make(items[n + 1])
        yield item, cur


def _flash_rows(m_ref, l_ref, acc_ref, rows, s, v, mask_fn=None, sub=64):
    r = s.shape[0]
    ps, alphas = [], []
    for u in range(r // sub):
        rr = slice(rows.start + u * sub, rows.start + (u + 1) * sub)
        su = s[u * sub:(u + 1) * sub]
        if mask_fn is not None:
            su = jnp.where(mask_fn(rr), su, NEG)
        sj = _lane_groups(su)
        m_prev = m_ref[rr, :]
        m_new = jnp.maximum(m_prev, jnp.max(functools.reduce(jnp.maximum, sj), axis=-1, keepdims=True))
        alpha = jnp.exp2(m_prev - m_new)
        pj = [jnp.exp2(x - m_new) for x in sj]
        l_ref[rr, :] = alpha * l_ref[rr, :] + jnp.sum(functools.reduce(jnp.add, pj), axis=-1, keepdims=True)
        m_ref[rr, :] = m_new
        ps.append(jnp.concatenate([x.astype(BF16) for x in pj], axis=1))
        alphas.append(alpha)
    pv = _dot(jnp.concatenate(ps, axis=0), v)
    acc_ref[rows, :] = jnp.concatenate(alphas, axis=0) * acc_ref[rows, :] + pv


def _softmax_weights(s, mask_fn, row0, sub=64):
    out = []
    for u in range(s.shape[0] // sub):
        su = jnp.where(mask_fn(slice(row0 + u * sub, row0 + (u + 1) * sub)), s[u * sub:(u + 1) * sub], -jnp.inf)
        sj = _lane_groups(su)
        m = jnp.max(functools.reduce(jnp.maximum, sj), axis=-1, keepdims=True)
        m = jnp.where(jnp.isfinite(m), m, 0.0)
        ej = [jnp.exp2(x - m) for x in sj]
        inv = 1.0 / jnp.maximum(jnp.sum(functools.reduce(jnp.add, ej), axis=-1, keepdims=True), 1e-30)
        out.append(jnp.concatenate([e * inv for e in ej], axis=1))
    return jnp.concatenate(out, axis=0)


def _softmax_pv_rows(s, v, mask_fn, row0, sub=64):
    r = s.shape[0]
    ps, ls = [], []
    for u in range(r // sub):
        su = jnp.where(mask_fn(slice(row0 + u * sub, row0 + (u + 1) * sub)), s[u * sub:(u + 1) * sub], NEG)
        sj = _lane_groups(su)
        m = jnp.max(functools.reduce(jnp.maximum, sj), axis=-1, keepdims=True)
        pj = [jnp.exp2(x - m) for x in sj]
        ls.append(jnp.broadcast_to(jnp.sum(functools.reduce(jnp.add, pj), axis=-1, keepdims=True), (sub, LANES)))
        ps.append(jnp.concatenate([x.astype(BF16) for x in pj], axis=1))
    return _dot(jnp.concatenate(ps, axis=0), v) / jnp.maximum(jnp.concatenate(ls, axis=0), 1e-30)


SEL_TK = 512
ROW_BLOCK = 256
WIN_KEYS = WINDOW + Q_BLOCK


def _attn_prompt_kernel(q_ref, gn_ref, selk_ref, selv_ref, wink_ref, winv_ref, kcvc_ref, csel_ref, pq_ref,
                        o_ref, q2_ref, m_ref, l_ref, acc_ref, oc_ref, ow_ref, ps_ref, s0_ref):
    i = pl.program_id(1)
    nrow = N_Q_HEADS * Q_BLOCK
    blocks = [slice(r * ROW_BLOCK, (r + 1) * ROW_BLOCK) for r in range(nrow // ROW_BLOCK)]

    def rel(rr, n, step=1):
        shape = (rr.stop - rr.start, n)
        return lax.broadcasted_iota(jnp.int32, shape, 1) * step - lax.broadcasted_iota(jnp.int32, shape, 0)

    def qpos0(rr):
        return i * Q_BLOCK + rr.start % Q_BLOCK

    qall = _dot(q_ref[...], pq_ref[...]).astype(BF16)
    for c in range(N_Q_HEADS):
        q2_ref[c * Q_BLOCK:(c + 1) * Q_BLOCK, 0:LANES] = qall[:, c * LANES:(c + 1) * LANES]

    kc = kcvc_ref[0, :, 0:LANES]
    vc = kcvc_ref[0, :, LANES:2 * LANES]
    n_c = kc.shape[0]
    heads = [slice(c * Q_BLOCK, (c + 1) * Q_BLOCK) for c in range(N_Q_HEADS)]
    for rows, s in _one_ahead(heads, lambda rr: _nt(q2_ref[rr, 0:LANES], kc)):
        c = rows.start // Q_BLOCK
        p_c = _softmax_weights(s, lambda rr: rel(rr, n_c, CMP_STRIDE) <= qpos0(rr) - (CMP_BLOCK - 1), rows.start)
        oc_ref[rows, :] = _dot(p_c.astype(BF16), vc)
        prow = slice((c // GROUP) * Q_BLOCK, (c // GROUP + 1) * Q_BLOCK)
        if c % GROUP == 0:
            ps_ref[prow, :] = p_c
        else:
            ps_ref[prow, :] += p_c

    imp_t = _split_dot(ps_ref[...], csel_ref[...]).T
    blk_t = lax.broadcasted_iota(jnp.int32, imp_t.shape, 0)
    qpos_t = i * Q_BLOCK + lax.broadcasted_iota(jnp.int32, imp_t.shape, 1) % Q_BLOCK
    forced_t = jnp.logical_or(blk_t == 0, blk_t == qpos_t // SEL_BLOCK)
    sel_t = _topk_mask_t(imp_t, forced_t, blk_t * SEL_BLOCK <= qpos_t, blk_t)
    bias = jnp.where(sel_t > 0.0, 0.0, NEG).T.astype(BF16)
    for c in range(N_Q_HEADS):
        h = c // GROUP
        q2_ref[c * Q_BLOCK:(c + 1) * Q_BLOCK, LANES:2 * LANES] = bias[h * Q_BLOCK:(h + 1) * Q_BLOCK]

    m_ref[...] = jnp.full(m_ref.shape, -jnp.inf, F32)
    l_ref[...] = jnp.zeros(l_ref.shape, F32)
    acc_ref[...] = jnp.zeros(acc_ref.shape, F32)

    def scores(rr, t):
        return _nt(q2_ref[rr, :], selk_ref[pl.ds(pl.multiple_of(t * SEL_TK, SEL_TK), SEL_TK), :])

    def sel_step(t, diagonal):
        k0 = pl.multiple_of(t * SEL_TK, SEL_TK)
        v2 = selv_ref[pl.ds(k0, SEL_TK), :]
        mask_fn = (lambda rr: rel(rr, SEL_TK) <= qpos0(rr) - k0) if diagonal else None
        pending = s0_ref[...]
        for n, rows in enumerate(blocks):
            s = pending
            if n + 1 < len(blocks):
                pending = scores(blocks[n + 1], t)
            elif not diagonal:
                s0_ref[...] = scores(blocks[0], t + 1)
            _flash_rows(m_ref, l_ref, acc_ref, rows, s, v2, mask_fn)

    n_below = (i * Q_BLOCK) // SEL_TK
    s0_ref[...] = scores(blocks[0], 0)

    def body(t, carry):
        sel_step(t, False)
        return carry

    lax.fori_loop(0, n_below, body, 0)
    sel_step(n_below, True)

    w0 = pl.multiple_of(jnp.maximum(i - WINDOW // Q_BLOCK, 0) * Q_BLOCK, Q_BLOCK)
    kw = wink_ref[pl.ds(w0, WIN_KEYS), :]
    vw = winv_ref[pl.ds(w0, WIN_KEYS), :]

    def win_mask(rr):
        dist = (qpos0(rr) - w0) - rel(rr, WIN_KEYS)
        return pltpu.bitcast(dist, jnp.uint32) < WINDOW

    for rows, s in _one_ahead(blocks, lambda rr: _nt(q2_ref[rr, 0:LANES], kw)):
        ow_ref[rows, :] = _softmax_pv_rows(s, vw, win_mask, rows.start)

    gn = gn_ref[...]
    for c in range(N_Q_HEADS):
        h = c // GROUP
        rows = slice(c * Q_BLOCK, (c + 1) * Q_BLOCK)
        o_s = acc_ref[rows, :] / jnp.maximum(l_ref[rows, :], 1e-30)
        o = (gn[:, 3 * c:3 * c + 1] * oc_ref[rows, :] + gn[:, 3 * c + 1:3 * c + 2] * o_s
             + gn[:, 3 * c + 2:3 * c + 3] * ow_ref[rows, :])
        o_ref[:, c * HEAD_DIM:(c + 1) * HEAD_DIM] = o[:, h * HEAD_DIM:(h + 1) * HEAD_DIM]


def _q_placement():
    p = np.zeros((D_ATTN, N_Q_HEADS * LANES), np.float32)
    for c in range(N_Q_HEADS):
        h = c // GROUP
        for d in range(HEAD_DIM):
            p[c * HEAD_DIM + d, c * LANES + h * HEAD_DIM + d] = 1.0
    return jnp.asarray(p, BF16)


def _attn_prompt(q, gn, selk, selv, wink, winv, kcvc, batch, seq_len):
    nq = seq_len // Q_BLOCK
    n_c = kcvc.shape[1]
    assert seq_len % SEL_TK == 0 and seq_len // SEL_BLOCK <= LANES and seq_len >= WIN_KEYS
    csel = jnp.asarray(_cmp_to_sel(n_c, n_c - 1, LANES), BF16)
    nrow = N_Q_HEADS * Q_BLOCK
    tile = lambda n: pl.BlockSpec((Q_BLOCK, n), lambda b, i: (b * nq + i, 0))
    seq = lambda n: pl.BlockSpec((seq_len, n), lambda b, i: (b, 0))
    return pl.pallas_call(
        _attn_prompt_kernel,
        grid=(batch, nq),
        in_specs=[tile(D_ATTN), tile(LANES), seq(2 * LANES), seq(LANES), seq(LANES), seq(LANES),
                  pl.BlockSpec((1, n_c, 256), lambda b, i: (b, 0, 0)),
                  _full(csel.shape), _full((D_ATTN, N_Q_HEADS * LANES))],
        out_specs=tile(D_ATTN),
        out_shape=jax.ShapeDtypeStruct((batch * seq_len, D_ATTN), F32),
        scratch_shapes=[pltpu.VMEM((nrow, 2 * LANES), BF16)] + [pltpu.VMEM((nrow, LANES), F32)] * 5
                       + [pltpu.VMEM((N_KV_HEADS * Q_BLOCK, n_c), F32), pltpu.VMEM((ROW_BLOCK, SEL_TK), F32)],
        compiler_params=_cparams(("parallel", "arbitrary")),
        name="attn_prompt",
    )(q, gn, selk, selv, wink, winv, kcvc, csel, _q_placement())


SAMPLE_PAGES_PER_STEP = 32
SAMPLE_TK = 512


def _attn_sample_kernel(pt_ref, q_ref, gn_ref, kcvc_ref, csel_ref, pq_ref, oh_ref, selk_new_ref, selv_new_ref,
                        wink_ref, winv_ref, pages_ref, o_ref,
                        buf_ref, sem_ref, q1_ref, bias_ref, oc_ref, m_ref, l_ref, acc_ref, ow_ref,
                        *, n_groups, gp, past_len, ls):
    b = pl.program_id(0)
    j = pl.program_id(1)
    nrow = N_Q_HEADS * ls
    tk = gp * 128
    blocks_per_step = tk // SEL_BLOCK

    def copies(jj, slot):
        return [pltpu.make_async_copy(pages_ref.at[pt_ref[b, jj * gp + p], :, pl.ds(256, 256)],
                                      buf_ref.at[slot, p], sem_ref.at[slot])
                for p in range(gp)]

    @pl.when(j == 0)
    def _():
        for c in copies(0, 0):
            c.start()

    slot = j % 2

    @pl.when(j + 1 < n_groups)
    def _():
        for c in copies(j + 1, 1 - slot):
            c.start()

    tok = lax.broadcasted_iota(jnp.int32, (nrow, 1), 0) % ls
    qpos = past_len + tok

    @pl.when(j == 0)
    def _():
        qall = _dot(q_ref[0], pq_ref[...]).astype(BF16)
        q1 = jnp.concatenate([qall[:, c * LANES:(c + 1) * LANES] for c in range(N_Q_HEADS)], axis=0)
        q1_ref[...] = q1
        kc = kcvc_ref[0, :, 0:LANES]
        vc = kcvc_ref[0, :, LANES:2 * LANES]
        n_c = kc.shape[0]
        cmp_end = lax.broadcasted_iota(jnp.int32, (1, n_c), 1) * CMP_STRIDE + (CMP_BLOCK - 1)
        p_c = _softmax2_rows(_nt(q1, kc), cmp_end <= qpos)
        oc_ref[...] = _dot(p_c.astype(BF16), vc)
        hrows = GROUP * ls
        psum = jnp.concatenate(
            [sum(p_c[h * hrows + g * ls:h * hrows + (g + 1) * ls] for g in range(GROUP))
             for h in range(N_KV_HEADS)], axis=0)
        imp_t = _split_dot(psum, csel_ref[...]).T
        blk_t = lax.broadcasted_iota(jnp.int32, imp_t.shape, 0)
        qpos_t = past_len + lax.broadcasted_iota(jnp.int32, imp_t.shape, 1) % ls
        forced_t = jnp.logical_or(blk_t == 0, blk_t == qpos_t // SEL_BLOCK)
        sel_t = _topk_mask_t(imp_t, forced_t, blk_t * SEL_BLOCK <= qpos_t, blk_t)
        bias = jnp.where(sel_t > 0.0, 0.0, NEG).T.astype(BF16)
        for h in range(N_KV_HEADS):
            for g in range(GROUP):
                r0 = (h * GROUP + g) * ls
                bias_ref[r0:r0 + ls, :] = bias[h * ls:(h + 1) * ls]
        kpos_new = past_len + lax.broadcasted_iota(jnp.int32, (1, ls), 1)
        s = jnp.where(kpos_new <= qpos, _nt(q1, selk_new_ref[0]), NEG)
        m0 = jnp.max(s, axis=-1, keepdims=True)
        p = jnp.exp2(s - m0)
        m_ref[...] = jnp.broadcast_to(m0, m_ref.shape)
        l_ref[...] = jnp.broadcast_to(jnp.sum(p, axis=-1, keepdims=True), l_ref.shape)
        acc_ref[...] = _dot(p.astype(BF16), selv_new_ref[0])
        n_w = wink_ref.shape[1]
        kpos_w = past_len + ls - n_w + lax.broadcasted_iota(jnp.int32, (1, n_w), 1)
        dist = qpos - kpos_w
        mask_w = jnp.logical_and(jnp.logical_and(kpos_w >= 0, dist >= 0), dist < WINDOW)
        p_w = _softmax2_rows(_nt(q1, wink_ref[0]), mask_w)
        ow_ref[...] = _dot(p_w.astype(BF16), winv_ref[0])

    for c in copies(j, slot):
        c.wait()

    b0 = pl.multiple_of((j * blocks_per_step // LANES) * LANES, LANES)
    q2 = jnp.concatenate([q1_ref[...], bias_ref[:, pl.ds(b0, LANES)]], axis=1)
    ppt = SAMPLE_TK // 128
    for t in range(gp // ppt):
        kv = buf_ref[slot, t * ppt:(t + 1) * ppt].reshape(SAMPLE_TK, 256)
        k2 = jnp.concatenate([kv[:, 0:LANES].astype(BF16), oh_ref[t * SAMPLE_TK:(t + 1) * SAMPLE_TK, :]], axis=1)
        v2 = kv[:, LANES:2 * LANES].astype(BF16)
        _flash_rows(m_ref, l_ref, acc_ref, slice(0, nrow), _nt(q2, k2), v2, sub=nrow)

    @pl.when(j == n_groups - 1)
    def _():
        gn = gn_ref[0]
        for c in range(N_Q_HEADS):
            h = c // GROUP
            rows = slice(c * ls, (c + 1) * ls)
            o_s = acc_ref[rows, :] / jnp.maximum(l_ref[rows, :], 1e-30)
            o = (gn[:, 3 * c:3 * c + 1] * oc_ref[rows, :] + gn[:, 3 * c + 1:3 * c + 2] * o_s
                 + gn[:, 3 * c + 2:3 * c + 3] * ow_ref[rows, :])
            o_ref[0, :, c * HEAD_DIM:(c + 1) * HEAD_DIM] = o[:, h * HEAD_DIM:(h + 1) * HEAD_DIM]


def _attn_sample(page_table, q, gn, kcvc, selk_new, selv_new, wink, winv, pages, past_len, ls):
    bs, n_pages = page_table.shape
    gp = min(SAMPLE_PAGES_PER_STEP, n_pages)
    n_groups = n_pages // gp
    tk = gp * 128
    blocks_per_step = tk // SEL_BLOCK
    assert LANES % blocks_per_step == 0 and tk % SAMPLE_TK == 0
    n_c = kcvc.shape[1]
    n_sel = -(-(past_len + ls) // SEL_BLOCK)
    n_sel_pad = -(-n_sel // LANES) * LANES
    csel = jnp.asarray(_cmp_to_sel(n_c, n_c - 1, n_sel_pad), BF16)
    key_blk = (np.arange(tk) // SEL_BLOCK)[:, None]
    onehot = [jnp.asarray((key_blk + s * blocks_per_step) % LANES == np.arange(LANES)[None, :], BF16)
              for s in range(LANES // blocks_per_step)]
    onehot = jnp.stack(onehot)
    n_rep = LANES // blocks_per_step
    nrow = N_Q_HEADS * ls
    kern = functools.partial(_attn_sample_kernel, n_groups=n_groups, gp=gp, past_len=past_len, ls=ls)
    per_seq = lambda *shape: pl.BlockSpec((1,) + shape, lambda b, j, pt: (b,) + (0,) * len(shape))
    const = lambda shape: pl.BlockSpec(shape, lambda b, j, pt: (0,) * len(shape))
    return pl.pallas_call(
        kern,
        grid_spec=pltpu.PrefetchScalarGridSpec(
            num_scalar_prefetch=1,
            grid=(bs, n_groups),
            in_specs=[per_seq(ls, D_ATTN), per_seq(ls, LANES), per_seq(n_c, 256),
                      const(csel.shape), const((D_ATTN, N_Q_HEADS * LANES)),
                      pl.BlockSpec((None, tk, LANES), lambda b, j, pt: (j % n_rep, 0, 0)),
                      per_seq(ls, LANES), per_seq(ls, LANES),
                      per_seq(wink.shape[1], LANES), per_seq(winv.shape[1], LANES),
                      pl.BlockSpec(memory_space=pl.ANY)],
            out_specs=per_seq(ls, D_ATTN),
            scratch_shapes=[pltpu.VMEM((2, gp, 128, 256), F32),
                            pltpu.SemaphoreType.DMA((2,)),
                            pltpu.VMEM((nrow, LANES), BF16),
                            pltpu.VMEM((nrow, n_sel_pad), BF16),
                            pltpu.VMEM((nrow, LANES), F32),
                            pltpu.VMEM((nrow, LANES), F32),
                            pltpu.VMEM((nrow, LANES), F32),
                            pltpu.VMEM((nrow, LANES), F32),
                            pltpu.VMEM((nrow, LANES), F32)]),
        out_shape=jax.ShapeDtypeStruct((bs, ls, D_ATTN), F32),
        compiler_params=_cparams(("arbitrary", "arbitrary")),
        name="attn_sample",
    )(page_table, q, gn, kcvc, csel, _q_placement(), onehot, selk_new, selv_new, wink, winv, pages)


def _cmul(ar, ai, br, bi):
    return ar * br - ai * bi, ar * bi + ai * br


def _ssm_weights(lam_re, lam_im, log_dt, b_re, b_im, c_re, c_im, tc):
    dt = jnp.exp(log_dt)[:, None]
    mag = jnp.exp(lam_re * dt)
    a_re = mag * jnp.cos(lam_im * dt)
    a_im = mag * jnp.sin(lam_im * dt)
    den = lam_re * lam_re + lam_im * lam_im
    f_re = ((a_re - 1.0) * lam_re + a_im * lam_im) / den
    f_im = (a_im * lam_re - (a_re - 1.0) * lam_im) / den
    bb_re = f_re[..., None] * b_re - f_im[..., None] * b_im
    bb_im = f_re[..., None] * b_im + f_im[..., None] * b_re
    pr, pi = jnp.ones_like(a_re)[None], jnp.zeros_like(a_re)[None]
    sr, si = a_re, a_im
    while pr.shape[0] < tc + 1:
        nr, ni = _cmul(pr, pi, sr[None], si[None])
        pr, pi = jnp.concatenate([pr, nr]), jnp.concatenate([pi, ni])
        sr, si = _cmul(sr, si, sr, si)
    pr, pi = pr[:tc + 1], pi[:tc + 1]
    cpr = c_re[None] * pr[:, :, None, :] - c_im[None] * pi[:, :, None, :]
    cpi = c_re[None] * pi[:, :, None, :] + c_im[None] * pr[:, :, None, :]
    kker = jnp.einsum('tgop,gpi->gtoi', cpr[:tc], bb_re) - jnp.einsum('tgop,gpi->gtoi', cpi[:tc], bb_im)
    lag = np.arange(tc)[None, :] - np.arange(tc)[:, None]
    kpad = jnp.concatenate([kker, jnp.zeros_like(kker[:, :1])], axis=1)
    toep = kpad[:, np.where(lag >= 0, lag, tc)]
    g = toep.shape[0]
    toep = jnp.transpose(toep, (0, 1, 4, 2, 3)).reshape(g, tc * SSM_GROUP, tc * SSM_GROUP)
    qr, qi = pr[:tc][::-1], pi[:tc][::-1]
    n_re = qr[..., None] * bb_re[None] - qi[..., None] * bb_im[None]
    n_im = qr[..., None] * bb_im[None] + qi[..., None] * bb_re[None]
    nmat = jnp.concatenate([n_re, n_im], axis=2)
    nmat = jnp.transpose(nmat, (1, 0, 3, 2)).reshape(g, tc * SSM_GROUP, 2 * SSM_STATE)
    m_re = cpr[1:tc + 1]
    m_im = -cpi[1:tc + 1]
    mmat = jnp.concatenate([m_re, m_im], axis=3)
    mmat = jnp.transpose(mmat, (1, 3, 0, 2)).reshape(g, 2 * SSM_STATE, tc * SSM_GROUP)
    return toep, nmat, mmat, pr[tc], pi[tc]


def _pair_expand_n(nmat):
    g, k, _ = nmat.shape
    out = jnp.zeros((g, k, 2, 2, SSM_STATE), nmat.dtype)
    par = np.arange(g) % 2
    n4 = nmat.reshape(g, k, 2, SSM_STATE)
    out = out.at[np.arange(g), :, :, par, :].set(n4)
    return out.reshape(g, k, 4 * SSM_STATE)


def _pair_expand_m(mmat):
    g, _, n = mmat.shape
    out = jnp.zeros((g, 2, 2, SSM_STATE, n), mmat.dtype)
    par = np.arange(g) % 2
    m4 = mmat.reshape(g, 2, SSM_STATE, n)
    out = out.at[np.arange(g), :, par, :, :].set(m4)
    return out.reshape(g, 4 * SSM_STATE, n)


def _ssm_state_kernel(u_ref, n_ref, sre_ref, sim_ref):
    s = _dot(u_ref[0], n_ref[0]) + _dot(u_ref[1], n_ref[1])
    sre_ref[...] = s[:, 0:LANES]
    sim_ref[...] = s[:, LANES:2 * LANES]


def _ssm_carry_kernel(sre_ref, sim_ref, h0re_ref, h0im_ref, are_ref, aim_ref,
                      hre_ref, him_ref, fre_ref, fim_ref):
    are = are_ref[...]
    aim = aim_ref[...]

    def body(k, carry):
        hr, hi = carry
        hre_ref[k] = hr
        him_ref[k] = hi
        nr = are * hr - aim * hi + sre_ref[k]
        ni = are * hi + aim * hr + sim_ref[k]
        return nr, ni

    hr, hi = lax.fori_loop(0, sre_ref.shape[0], body, (h0re_ref[...], h0im_ref[...]))
    fre_ref[...] = hr
    fim_ref[...] = hi


def _ssm_out_kernel(u_ref, t_ref, m_ref, hre_ref, him_ref, y_ref):
    hp = jnp.concatenate([hre_ref[...], him_ref[...]], axis=1).astype(BF16)
    for e in range(2):
        y_ref[e] = _dot(u_ref[e], t_ref[e]) + _dot(hp, m_ref[e])


def _ssm(u2d, batch, seq_len, tc, h0_re, h0_im, ssm_w):
    toep, nmat, mmat, at_re, at_im = ssm_w
    g = toep.shape[0]
    nck = seq_len // tc
    r = nck * batch
    kk = tc * SSM_GROUP
    ug = u2d.astype(BF16).reshape(batch, nck, tc, g, SSM_GROUP)
    ug = jnp.transpose(ug, (3, 1, 0, 2, 4)).reshape(g, r, kk)
    gs = g * SSM_STATE
    pair = lambda *shape: pl.BlockSpec((2,) + shape, lambda j: (j,) + (0,) * len(shape))
    col = pl.BlockSpec((r, LANES), lambda j: (0, j))
    s_re, s_im = pl.pallas_call(
        _ssm_state_kernel,
        grid=(g // 2,),
        in_specs=[pair(r, kk), pair(kk, 4 * SSM_STATE)],
        out_specs=[col, col],
        out_shape=[jax.ShapeDtypeStruct((r, gs), F32)] * 2,
        compiler_params=_cparams(("parallel",)),
        name="ssm_state",
    )(ug, _pair_expand_n(nmat).astype(BF16))
    cw = 512
    seq3 = pl.BlockSpec((nck, batch, cw), lambda c: (0, 0, c))
    vec = lambda n: pl.BlockSpec((n, cw), lambda c: (0, c))
    h_re, h_im, f_re, f_im = pl.pallas_call(
        _ssm_carry_kernel,
        grid=(gs // cw,),
        in_specs=[seq3, seq3, vec(batch), vec(batch), vec(1), vec(1)],
        out_specs=[seq3, seq3, vec(batch), vec(batch)],
        out_shape=[jax.ShapeDtypeStruct((nck, batch, gs), F32)] * 2 + [jax.ShapeDtypeStruct((batch, gs), F32)] * 2,
        compiler_params=_cparams(("parallel",)),
        name="ssm_carry",
    )(s_re.reshape(nck, batch, gs), s_im.reshape(nck, batch, gs), h0_re.reshape(batch, gs),
      h0_im.reshape(batch, gs), at_re.reshape(1, gs), at_im.reshape(1, gs))
    h_re, h_im = h_re.reshape(r, gs), h_im.reshape(r, gs)
    y = pl.pallas_call(
        _ssm_out_kernel,
        grid=(g // 2,),
        in_specs=[pair(r, kk), pair(kk, kk), pair(4 * SSM_STATE, kk), col, col],
        out_specs=pair(r, kk),
        out_shape=jax.ShapeDtypeStruct((g, r, kk), F32),
        compiler_params=_cparams(("parallel",)),
        name="ssm_out",
    )(ug, toep.astype(BF16), _pair_expand_m(mmat).astype(BF16), h_re, h_im)
    y = jnp.transpose(y.reshape(g, nck, batch, tc, SSM_GROUP), (2, 1, 3, 0, 4)).reshape(batch * seq_len, g * SSM_GROUP)
    return y, f_re.reshape(batch, g, SSM_STATE), f_im.reshape(batch, g, SSM_STATE)


def _post_kernel(x_ref, oa_ref, ys_ref, u_ref, g_ref, fg_ref, d_ref, bglu_ref,
                 wza_ref, wzs_ref, wgm_ref, wglu_ref, wla_ref, wls_ref, wo_ref, out_ref):
    x = x_ref[...]
    d = x.shape[1]
    xn = x * lax.rsqrt(jnp.mean(x * x, axis=-1, keepdims=True) + RMS_EPS) * g_ref[...]
    xb = xn.astype(BF16)
    z_a = _dot(xb, wza_ref[...])
    z_s = _dot(xb, wzs_ref[...])
    gm = jax.nn.sigmoid(_dot(xb, wgm_ref[...]))
    branch_a = _dot((oa_ref[...] * jax.nn.silu(z_a)).astype(BF16), wla_ref[...])
    y = jax.nn.gelu(ys_ref[...] + d_ref[...] * u_ref[...])
    y = y * jax.nn.sigmoid(_dot(y.astype(BF16), wglu_ref[...]) + bglu_ref[...])
    branch_b = _dot((y * jax.nn.silu(z_s)).astype(BF16), wls_ref[...])
    merged = gm[:, :d] * branch_a + gm[:, d:] * branch_b
    r = x + _dot(merged.astype(BF16), wo_ref[...])
    out_ref[...] = r * lax.rsqrt(jnp.mean(r * r, axis=-1, keepdims=True) + RMS_EPS) * fg_ref[...]


def _post(x2d, o_attn, y_ssm, u, norm_g, final_g, ssm_d, b_glu, wza, wzs, wgm, wglu, wla, wls, wo):
    t, d = x2d.shape
    tm = min(256, t)
    row = lambda n: pl.BlockSpec((tm, n), lambda i: (i, 0))
    ws = [wza, wzs, wgm, wglu, wla, wls, wo]
    vecs = [norm_g.reshape(1, d), final_g.reshape(1, d), ssm_d.reshape(1, -1), b_glu.reshape(1, -1)]
    return pl.pallas_call(
        _post_kernel,
        grid=(t // tm,),
        in_specs=[row(d), row(o_attn.shape[1]), row(y_ssm.shape[1]), row(u.shape[1])]
                 + [_full(v.shape) for v in vecs] + [_full(w.shape) for w in ws],
        out_specs=row(d),
        out_shape=jax.ShapeDtypeStruct((t, d), F32),
        compiler_params=_cparams(("parallel",)),
        name="post",
    )(x2d, o_attn, y_ssm, u, *vecs, *ws)


SSM_CHUNK_PROMPT = 64


def kernel(x_prompt, x_sample, cache_kv, cache_win_kv, state_ssm_re, state_ssm_im, page_table, norm_g, w_in, cmp_pe, cmp_w1, cmp_b1, cmp_w2, ssm_lam_re, ssm_lam_im, ssm_log_dt, ssm_b_re, ssm_b_im, ssm_c_re, ssm_c_im, ssm_d, w_glu, b_glu, w_lift_attn, w_lift_ssm, w_out, final_g):
    depth = norm_g.shape[0]
    assert depth == 1
    l = 0
    bp, lp, d = x_prompt.shape
    bs, ls, _ = x_sample.shape
    n_pages, page = page_table.shape[1], cache_kv.shape[2]
    past_len = n_pages * page
    w_buf = cache_win_kv.shape[2]
    d_ssm = ssm_d.shape[1]
    n_grp = d_ssm // SSM_GROUP
    assert page == 128 and w_buf == WINDOW and lp % Q_BLOCK == 0 and lp >= WINDOW

    splits = (D_ATTN, N_KV_SLOTS * N_KV_HEADS * HEAD_DIM, 3 * N_Q_HEADS, D_ATTN, d_ssm, d_ssm, 2 * d)
    offs = np.concatenate([[0], np.cumsum(splits)])
    wb = w_in[l].astype(BF16)
    wq, wkv, wg, wza, wu, wzs, wgm = [wb[:, offs[k]:offs[k + 1]] for k in range(7)]
    wg = jnp.pad(wg, ((0, 0), (0, LANES - wg.shape[1])))
    cw1, cb, cw2 = _compress_weights(cmp_pe[l], cmp_w1[l], cmp_b1[l], cmp_w2[l])
    ssm_p = (ssm_lam_re[l], ssm_lam_im[l], ssm_log_dt[l], ssm_b_re[l], ssm_b_im[l], ssm_c_re[l], ssm_c_im[l])
    post_w = (norm_g[l], final_g, ssm_d[l], b_glu[l], wza, wzs, wgm, w_glu[l].astype(BF16),
              w_lift_attn[l].astype(BF16), w_lift_ssm[l].astype(BF16), w_out[l].astype(BF16))

    xp = x_prompt.reshape(bp * lp, d)
    q, kv, win, selk, selv, wink, winv, gn, u = _pre(xp, norm_g[l], wq, wkv, wg, wu, lp)
    ident = jnp.arange(bp * (lp // 128), dtype=jnp.int32).reshape(bp, lp // 128)
    kcvc = _compress(ident, kv.reshape(bp * lp // 128, 128, KV_ROW), cw1, cb, cw2)
    o_attn = _attn_prompt(q, gn, selk, selv, wink, winv, kcvc, bp, lp)
    tcp = math.gcd(lp, SSM_CHUNK_PROMPT)
    h0 = jnp.zeros((bp, n_grp, SSM_STATE), F32)
    y_ssm, hr_p, hi_p = _ssm(u, bp, lp, tcp, h0, h0, _ssm_weights(*ssm_p, tcp))
    y_prompt = _post(xp, o_attn, y_ssm, u, *post_w).reshape(bp, lp, d)
    kv_prompt = kv.reshape(1, bp, lp, 4, N_KV_HEADS, HEAD_DIM)
    win_prompt = win.reshape(bp, lp, 2, N_KV_HEADS, HEAD_DIM)[None, :, lp - WINDOW:]

    xs = x_sample.reshape(bs * ls, d)
    q, kv, win, selk, selv, wink, winv, gn, u = _pre(xs, norm_g[l], wq, wkv, wg, wu, ls)
    pages = cache_kv[l].reshape(-1, 128, KV_ROW)
    kcvc = _compress(page_table, pages, cw1, cb, cw2)
    cwin = cache_win_kv[l].reshape(bs, w_buf, 2 * LANES)
    wk_all = jnp.concatenate([cwin[:, :, :LANES].astype(BF16), wink.reshape(bs, ls, LANES)], axis=1)
    wv_all = jnp.concatenate([cwin[:, :, LANES:].astype(BF16), winv.reshape(bs, ls, LANES)], axis=1)
    o_attn = _attn_sample(page_table, q.reshape(bs, ls, D_ATTN), gn.reshape(bs, ls, LANES), kcvc,
                          selk[:, :LANES].reshape(bs, ls, LANES), selv.reshape(bs, ls, LANES),
                          wk_all, wv_all, pages, past_len, ls)
    y_ssm, hr_s, hi_s = _ssm(u, bs, ls, ls, state_ssm_re[l], state_ssm_im[l], _ssm_weights(*ssm_p, ls))
    y_sample = _post(xs, o_attn.reshape(bs * ls, D_ATTN), y_ssm, u, *post_w).reshape(bs, ls, d)
    kv_sample = kv.reshape(1, bs, ls, 4, N_KV_HEADS, HEAD_DIM)
    win_new = win.reshape(bs, ls, 2, N_KV_HEADS, HEAD_DIM)
    win_sample = jnp.concatenate([cache_win_kv[l], win_new], axis=1)[None, :, ls:]

    return (y_prompt, y_sample, kv_prompt, win_prompt, hr_p[None], hi_p[None],
            kv_sample, win_sample, hr_s[None], hi_s[None])
```

```python
import functools
import math

import numpy as np
import jax
import jax.numpy as jnp
from jax import lax
from jax.experimental import pallas as pl
from jax.experimental.pallas import tpu as pltpu

F32 = jnp.float32
BF16 = jnp.bfloat16

HEAD_DIM = 64
N_KV_HEADS = 2
GROUP = 4
N_Q_HEADS = N_KV_HEADS * GROUP
D_ATTN = N_Q_HEADS * HEAD_DIM
N_KV_SLOTS = 6
CMP_BLOCK = 32
CMP_STRIDE = 16
SEL_BLOCK = 64
N_SEL = 16
WINDOW = 512
Q_BLOCK = 128
SSM_GROUP = 16
SSM_STATE = 64
SSM_SET = 8
RMS_EPS = 1e-6
LANES = 128
KV_ROW = 4 * N_KV_HEADS * HEAD_DIM
NEG = -1e30
LOG2E = math.log2(math.e)
VMEM_LIMIT = 56 * 1024 * 1024


def _cparams(sem):
    return pltpu.CompilerParams(dimension_semantics=sem, vmem_limit_bytes=VMEM_LIMIT)


def _full(shape):
    nd = len(shape)
    return pl.BlockSpec(shape, lambda *_: (0,) * nd)


def _nt(a, b):
    return lax.dot_general(a, b, (((1,), (1,)), ((), ())), preferred_element_type=F32)


def _dot(a, b):
    return jnp.dot(a, b, preferred_element_type=F32)


def _pre_kernel(x_ref, g_ref, wq_ref, wkv_ref, wg_ref, wu_ref,
                q_ref, kv_ref, win_ref, cmp_ref, selk_ref, selv_ref, wink_ref, winv_ref, gn_ref, u_ref,
                *, seq_len, tm, transposed_kv):
    x = x_ref[...]
    xn = x * lax.rsqrt(jnp.mean(x * x, axis=-1, keepdims=True) + RMS_EPS) * g_ref[...]
    xb = xn.astype(BF16)
    q = _dot(xb, wq_ref[...])
    q_ref[...] = (q * (LOG2E * HEAD_DIM ** -0.5)).astype(BF16)
    kv = _dot(xb, wkv_ref[...])
    if transposed_kv:
        for c in range(4):
            kv_ref[c * LANES:(c + 1) * LANES, :] = kv[:, c * LANES:(c + 1) * LANES].T
        for c in range(2):
            win_ref[c * LANES:(c + 1) * LANES, :] = kv[:, KV_ROW + c * LANES:KV_ROW + (c + 1) * LANES].T
    else:
        kv_ref[...] = kv[:, :KV_ROW]
        win_ref[...] = kv[:, KV_ROW:]
    cmp_ref[...] = kv[:, 0:2 * LANES]
    pos = (pl.program_id(0) * tm + lax.broadcasted_iota(jnp.int32, (tm, LANES), 0)) % seq_len
    onehot = lax.broadcasted_iota(jnp.int32, (tm, LANES), 1) == (pos // SEL_BLOCK) % LANES
    selk_ref[:, :LANES] = kv[:, 256:384].astype(BF16)
    selk_ref[:, LANES:] = jnp.where(onehot, 1.0, 0.0).astype(BF16)
    selv_ref[...] = kv[:, 384:512].astype(BF16)
    wink_ref[...] = kv[:, 512:640].astype(BF16)
    winv_ref[...] = kv[:, 640:768].astype(BF16)
    gn_ref[...] = jax.nn.sigmoid(_dot(xb, wg_ref[...]))
    u_ref[...] = _dot(xb, wu_ref[...])


def _pre(x2d, norm_g, wq, wkv, wg, wu, seq_len, transposed_kv):
    t, d = x2d.shape
    tm = min(256, t)
    kern = functools.partial(_pre_kernel, seq_len=seq_len, tm=tm, transposed_kv=transposed_kv)
    row = lambda n: pl.BlockSpec((tm, n), lambda i: (i, 0))
    outs = [(D_ATTN, BF16), (KV_ROW, F32), (2 * LANES, F32), (2 * LANES, F32), (2 * LANES, BF16), (LANES, BF16),
            (LANES, BF16), (LANES, BF16), (LANES, F32), (wu.shape[1], F32)]
    out_specs = [row(n) for n, _ in outs]
    out_shape = [jax.ShapeDtypeStruct((t, n), dt) for n, dt in outs]
    if transposed_kv:
        per_seq = seq_len // tm
        for k in (1, 2):
            n = outs[k][0]
            out_specs[k] = pl.BlockSpec((None, n, tm), lambda i: (i // per_seq, 0, i % per_seq))
            out_shape[k] = jax.ShapeDtypeStruct((t // seq_len, n, seq_len), F32)
    return pl.pallas_call(
        kern,
        grid=(t // tm,),
        in_specs=[row(d), _full((1, d)), _full(wq.shape), _full(wkv.shape), _full(wg.shape), _full(wu.shape)],
        out_specs=out_specs,
        out_shape=out_shape,
        compiler_params=_cparams(("parallel",)),
        name="pre",
    )(x2d, norm_g.reshape(1, d), wq, wkv, wg, wu)


PAGES_PER_STEP = 32


def _compress_kernel(pt_ref, pages_ref, w1_ref, b_ref, w2_ref, out_ref, buf_ref, sem_ref, h_ref, *tbuf_ref,
                     n_groups, gp, transposed):
    b = pl.program_id(0)
    j = pl.program_id(1)
    rows = gp * (128 // CMP_STRIDE)

    def copies(jj, slot):
        if transposed:
            return [pltpu.make_async_copy(pages_ref.at[pt_ref[b, jj * gp + p], pl.ds(0, 2)],
                                          tbuf_ref[0].at[slot, p], sem_ref.at[slot])
                    for p in range(gp)]
        return [pltpu.make_async_copy(pages_ref.at[pt_ref[b, jj * gp + p], :, pl.ds(half * LANES, LANES)],
                                      buf_ref.at[slot, half, p], sem_ref.at[slot])
                for p in range(gp) for half in range(2)]

    @pl.when(j == 0)
    def _():
        for c in copies(0, 0):
            c.start()

    slot = j % 2

    @pl.when(j + 1 < n_groups)
    def _():
        for c in copies(j + 1, 1 - slot):
            c.start()

    for c in copies(j, slot):
        c.wait()

    if transposed:
        for p in range(gp):
            for half in range(2):
                buf_ref[0, half, p] = tbuf_ref[0][slot, p, half].reshape(LANES, 128).T
        slot = 0

    acc = jnp.zeros((rows, 512), F32)
    for jj in range(CMP_STRIDE):
        xj = [buf_ref[slot, half, :, pl.ds(jj, 128 // CMP_STRIDE, stride=CMP_STRIDE), :].reshape(rows, LANES)
              for half in range(2)]
        xj = jnp.concatenate(xj, axis=1).astype(BF16)
        acc = acc + _dot(xj, w1_ref[jj])
    h_ref[pl.ds(pl.multiple_of(j * rows, rows), rows), :] = acc

    @pl.when(j == n_groups - 1)
    def _():
        n = h_ref.shape[0]
        hb = pltpu.roll(h_ref[:, 256:512], n - 1, 0)
        hid = h_ref[:, 0:256] + hb + b_ref[...]
        act = jax.nn.gelu(hid).astype(BF16)
        out_ref[0] = _dot(act, w2_ref[...]).astype(BF16)


def _compress(page_table, pages, w1, b_eff, w2):
    n_seq, n_pages = page_table.shape
    gp = min(PAGES_PER_STEP, n_pages)
    n_groups = n_pages // gp
    n_chunks = n_pages * (128 // CMP_STRIDE)
    transposed = pages.ndim == 5
    kern = functools.partial(_compress_kernel, n_groups=n_groups, gp=gp, transposed=transposed)
    scratch = [pltpu.VMEM((1 if transposed else 2, 2, gp, 128, LANES), F32),
               pltpu.SemaphoreType.DMA((2,)),
               pltpu.VMEM((n_chunks, 512), F32)]
    if transposed:
        scratch.append(pltpu.VMEM((2, gp, 2, N_KV_HEADS, HEAD_DIM, 128), F32))
    return pl.pallas_call(
        kern,
        grid_spec=pltpu.PrefetchScalarGridSpec(
            num_scalar_prefetch=1,
            grid=(n_seq, n_groups),
            in_specs=[pl.BlockSpec(memory_space=pl.ANY),
                      pl.BlockSpec(w1.shape, lambda b, j, pt: (0, 0, 0)),
                      pl.BlockSpec(b_eff.shape, lambda b, j, pt: (0, 0)),
                      pl.BlockSpec(w2.shape, lambda b, j, pt: (0, 0))],
            out_specs=pl.BlockSpec((1, n_chunks, 256), lambda b, j, pt: (b, 0, 0)),
            scratch_shapes=scratch),
        out_shape=jax.ShapeDtypeStruct((n_seq, n_chunks, 256), BF16),
        compiler_params=_cparams(("arbitrary", "arbitrary")),
        name="compress",
    )(page_table, pages, w1, b_eff, w2)


def _compress_weights(cmp_pe, cmp_w1, cmp_b1, cmp_w2):
    w1 = cmp_w1.reshape(2, 2, CMP_STRIDE, HEAD_DIM, HEAD_DIM)
    eye = jnp.eye(N_KV_HEADS, dtype=F32)
    blk = jnp.einsum('vrjde,vw,hg->rjvhdwge', w1, jnp.eye(2, dtype=F32), eye)
    blk = blk.reshape(2, CMP_STRIDE, 256, 256)
    w1_all = jnp.concatenate([blk[0], blk[1]], axis=-1).astype(BF16)
    pe = cmp_pe.reshape(2, 2, CMP_STRIDE, HEAD_DIM)
    pe_bias = jnp.einsum('vrjd,vrjde->ve', pe, w1) + cmp_b1
    b_eff = jnp.broadcast_to(pe_bias[:, None, :], (2, N_KV_HEADS, HEAD_DIM)).reshape(1, 256)
    w2 = jnp.einsum('ved,vw,hg->vhewgd', cmp_w2, jnp.eye(2, dtype=F32), eye).reshape(256, 256).astype(BF16)
    return w1_all, b_eff, w2


def _cmp_to_sel(n_cmp_rows, n_valid, n_sel_pad):
    c0 = np.arange(n_cmp_rows)[:, None] * CMP_STRIDE
    s0 = np.arange(n_sel_pad)[None, :] * SEL_BLOCK
    shared = np.minimum(c0 + CMP_BLOCK, s0 + SEL_BLOCK) - np.maximum(c0, s0)
    m = np.clip(shared, 0, None).astype(np.float32) / CMP_BLOCK
    m[n_valid:] = 0.0
    return m


def _softmax2_rows(s, mask):
    s = jnp.where(mask, s, -jnp.inf)
    m = jnp.max(s, axis=-1, keepdims=True)
    m = jnp.where(jnp.isfinite(m), m, 0.0)
    e = jnp.exp2(s - m)
    return e / jnp.maximum(jnp.sum(e, axis=-1, keepdims=True), 1e-30)


def _split_dot(p, w):
    hi = p.astype(BF16)
    lo = (p - hi.astype(F32)).astype(BF16)
    return _dot(hi, w) + _dot(lo, w)


def _topk_mask_t(imp_t, forced_t, causal_t, blk_t):
    v = jnp.where(forced_t, jnp.inf, jnp.where(causal_t, imp_t, -jnp.inf))
    big = jnp.int32(imp_t.shape[0])
    sel = jnp.zeros(imp_t.shape, F32)
    for _ in range(N_SEL):
        mx = jnp.max(v, axis=0, keepdims=True)
        idx = jnp.min(jnp.where(v == mx, blk_t, big), axis=0, keepdims=True)
        pick = blk_t == idx
        sel = jnp.where(pick, jnp.maximum(sel, jnp.where(mx > -jnp.inf, 1.0, 0.0)), sel)
        v = jnp.where(pick, -jnp.inf, v)
    return sel


def _lane_groups(s):
    return [s[:, j * LANES:(j + 1) * LANES] for j in range(s.shape[1] // LANES)]


def _one_ahead(items, make):
    nxt = make(items[0])
    for n, item in enumerate(items):
        cur = nxt
        if n + 1 < len(items):
            nxt = make(items[n + 1])
        yield item, cur


def _flash_rows(m_ref, l_ref, acc_ref, rows, s, v, mask_fn=None, sub=64, v_transposed=False):
    r = s.shape[0]
    ps, alphas = [], []
    for u in range(r // sub):
        rr = slice(rows.start + u * sub, rows.start + (u + 1) * sub)
        su = s[u * sub:(u + 1) * sub]
        if mask_fn is not None:
            su = jnp.where(mask_fn(rr), su, NEG)
        sj = _lane_groups(su)
        m_prev = m_ref[rr, :]
        m_new = jnp.maximum(m_prev, jnp.max(functools.reduce(jnp.maximum, sj), axis=-1, keepdims=True))
        alpha = jnp.exp2(m_prev - m_new)
        pj = [jnp.exp2(x - m_new) for x in sj]
        l_ref[rr, :] = alpha * l_ref[rr, :] + jnp.sum(functools.reduce(jnp.add, pj), axis=-1, keepdims=True)
        m_ref[rr, :] = m_new
        ps.append(jnp.concatenate([x.astype(BF16) for x in pj], axis=1))
        alphas.append(alpha)
    p = jnp.concatenate(ps, axis=0)
    pv = _nt(p, v) if v_transposed else _dot(p, v)
    acc_ref[rows, :] = jnp.concatenate(alphas, axis=0) * acc_ref[rows, :] + pv


def _softmax_weights(s, mask_fn, row0, sub=64):
    out = []
    for u in range(s.shape[0] // sub):
        su = jnp.where(mask_fn(slice(row0 + u * sub, row0 + (u + 1) * sub)), s[u * sub:(u + 1) * sub], -jnp.inf)
        sj = _lane_groups(su)
        m = jnp.max(functools.reduce(jnp.maximum, sj), axis=-1, keepdims=True)
        m = jnp.where(jnp.isfinite(m), m, 0.0)
        ej = [jnp.exp2(x - m) for x in sj]
        inv = 1.0 / jnp.maximum(jnp.sum(functools.reduce(jnp.add, ej), axis=-1, keepdims=True), 1e-30)
        out.append(jnp.concatenate([e * inv for e in ej], axis=1))
    return jnp.concatenate(out, axis=0)


def _softmax_pv_rows(s, v, mask_fn, row0, sub=64):
    r = s.shape[0]
    ps, ls = [], []
    for u in range(r // sub):
        su = jnp.where(mask_fn(slice(row0 + u * sub, row0 + (u + 1) * sub)), s[u * sub:(u + 1) * sub], NEG)
        sj = _lane_groups(su)
        m = jnp.max(functools.reduce(jnp.maximum, sj), axis=-1, keepdims=True)
        pj = [jnp.exp2(x - m) for x in sj]
        ls.append(jnp.broadcast_to(jnp.sum(functools.reduce(jnp.add, pj), axis=-1, keepdims=True), (sub, LANES)))
        ps.append(jnp.concatenate([x.astype(BF16) for x in pj], axis=1))
    return _dot(jnp.concatenate(ps, axis=0), v) / jnp.maximum(jnp.concatenate(ls, axis=0), 1e-30)


SEL_TK = 512
ROW_BLOCK = 512
WIN_ROW_BLOCK = 512
WIN_KEYS = WINDOW + Q_BLOCK


def _attn_prompt_kernel(q_ref, gn_ref, selk_ref, selv_ref, wink_ref, winv_ref, kcvc_ref, csel_ref, pq_ref,
                        o_ref, q2_ref, m_ref, l_ref, acc_ref, oc_ref, ow_ref, ps_ref, s0_ref):
    i = pl.program_id(1)
    nrow = N_Q_HEADS * Q_BLOCK
    blocks = [slice(r * ROW_BLOCK, (r + 1) * ROW_BLOCK) for r in range(nrow // ROW_BLOCK)]

    def rel(rr, n, step=1):
        shape = (rr.stop - rr.start, n)
        return lax.broadcasted_iota(jnp.int32, shape, 1) * step - lax.broadcasted_iota(jnp.int32, shape, 0)

    def qpos0(rr):
        return i * Q_BLOCK + rr.start % Q_BLOCK

    qall = _dot(q_ref[...], pq_ref[...]).astype(BF16)
    for c in range(N_Q_HEADS):
        q2_ref[c * Q_BLOCK:(c + 1) * Q_BLOCK, 0:LANES] = qall[:, c * LANES:(c + 1) * LANES]

    kc = kcvc_ref[0, :, 0:LANES]
    vc = kcvc_ref[0, :, LANES:2 * LANES]
    n_c = kc.shape[0]
    heads = [slice(c * Q_BLOCK, (c + 1) * Q_BLOCK) for c in range(N_Q_HEADS)]
    for rows, s in _one_ahead(heads, lambda rr: _nt(q2_ref[rr, 0:LANES], kc)):
        c = rows.start // Q_BLOCK
        p_c = _softmax_weights(s, lambda rr: rel(rr, n_c, CMP_STRIDE) <= qpos0(rr) - (CMP_BLOCK - 1), rows.start)
        oc_ref[rows, :] = _dot(p_c.astype(BF16), vc)
        prow = slice((c // GROUP) * Q_BLOCK, (c // GROUP + 1) * Q_BLOCK)
        if c % GROUP == 0:
            ps_ref[prow, :] = p_c
        else:
            ps_ref[prow, :] += p_c

    imp_t = _split_dot(ps_ref[...], csel_ref[...]).T
    blk_t = lax.broadcasted_iota(jnp.int32, imp_t.shape, 0)
    qpos_t = i * Q_BLOCK + lax.broadcasted_iota(jnp.int32, imp_t.shape, 1) % Q_BLOCK
    forced_t = jnp.logical_or(blk_t == 0, blk_t == qpos_t // SEL_BLOCK)
    sel_t = _topk_mask_t(imp_t, forced_t, blk_t * SEL_BLOCK <= qpos_t, blk_t)
    bias = jnp.where(sel_t > 0.0, 0.0, NEG).T.astype(BF16)
    for c in range(N_Q_HEADS):
        h = c // GROUP
        q2_ref[c * Q_BLOCK:(c + 1) * Q_BLOCK, LANES:2 * LANES] = bias[h * Q_BLOCK:(h + 1) * Q_BLOCK]

    m_ref[...] = jnp.full(m_ref.shape, -jnp.inf, F32)
    l_ref[...] = jnp.zeros(l_ref.shape, F32)
    acc_ref[...] = jnp.zeros(acc_ref.shape, F32)

    def scores(rr, t):
        return _nt(q2_ref[rr, :], selk_ref[pl.ds(pl.multiple_of(t * SEL_TK, SEL_TK), SEL_TK), :])

    def sel_step(t, diagonal):
        k0 = pl.multiple_of(t * SEL_TK, SEL_TK)
        v2 = selv_ref[pl.ds(k0, SEL_TK), :]
        mask_fn = (lambda rr: rel(rr, SEL_TK) <= qpos0(rr) - k0) if diagonal else None
        pending = s0_ref[...]
        for n, rows in enumerate(blocks):
            s = pending
            if n + 1 < len(blocks):
                pending = scores(blocks[n + 1], t)
            elif not diagonal:
                s0_ref[...] = scores(blocks[0], t + 1)
            _flash_rows(m_ref, l_ref, acc_ref, rows, s, v2, mask_fn)

    n_below = (i * Q_BLOCK) // SEL_TK
    s0_ref[...] = scores(blocks[0], 0)

    def body(t, carry):
        sel_step(t, False)
        return carry

    lax.fori_loop(0, n_below, body, 0)
    sel_step(n_below, True)

    w0 = pl.multiple_of(jnp.maximum(i - WINDOW // Q_BLOCK, 0) * Q_BLOCK, Q_BLOCK)
    kw = wink_ref[pl.ds(w0, WIN_KEYS), :]
    vw = winv_ref[pl.ds(w0, WIN_KEYS), :]

    def win_mask(rr):
        dist = (qpos0(rr) - w0) - rel(rr, WIN_KEYS)
        return pltpu.bitcast(dist, jnp.uint32) < WINDOW

    wblocks = [slice(r * WIN_ROW_BLOCK, (r + 1) * WIN_ROW_BLOCK) for r in range(nrow // WIN_ROW_BLOCK)]
    for rows, s in _one_ahead(wblocks, lambda rr: _nt(q2_ref[rr, 0:LANES], kw)):
        ow_ref[rows, :] = _softmax_pv_rows(s, vw, win_mask, rows.start)

    gn = gn_ref[...]
    for c in range(N_Q_HEADS):
        h = c // GROUP
        rows = slice(c * Q_BLOCK, (c + 1) * Q_BLOCK)
        o_s = acc_ref[rows, :] / jnp.maximum(l_ref[rows, :], 1e-30)
        o = (gn[:, 3 * c:3 * c + 1] * oc_ref[rows, :] + gn[:, 3 * c + 1:3 * c + 2] * o_s
             + gn[:, 3 * c + 2:3 * c + 3] * ow_ref[rows, :])
        o_ref[:, c * HEAD_DIM:(c + 1) * HEAD_DIM] = o[:, h * HEAD_DIM:(h + 1) * HEAD_DIM]


def _q_placement():
    p = np.zeros((D_ATTN, N_Q_HEADS * LANES), np.float32)
    for c in range(N_Q_HEADS):
        h = c // GROUP
        for d in range(HEAD_DIM):
            p[c * HEAD_DIM + d, c * LANES + h * HEAD_DIM + d] = 1.0
    return jnp.asarray(p, BF16)


def _attn_prompt(q, gn, selk, selv, wink, winv, kcvc, batch, seq_len):
    nq = seq_len // Q_BLOCK
    n_c = kcvc.shape[1]
    assert seq_len % SEL_TK == 0 and seq_len // SEL_BLOCK <= LANES and seq_len >= WIN_KEYS
    csel = jnp.asarray(_cmp_to_sel(n_c, n_c - 1, LANES), BF16)
    nrow = N_Q_HEADS * Q_BLOCK
    tile = lambda n: pl.BlockSpec((Q_BLOCK, n), lambda b, i: (b * nq + i, 0))
    seq = lambda n: pl.BlockSpec((seq_len, n), lambda b, i: (b, 0))
    return pl.pallas_call(
        _attn_prompt_kernel,
        grid=(batch, nq),
        in_specs=[tile(D_ATTN), tile(LANES), seq(2 * LANES), seq(LANES), seq(LANES), seq(LANES),
                  pl.BlockSpec((1, n_c, 256), lambda b, i: (b, 0, 0)),
                  _full(csel.shape), _full((D_ATTN, N_Q_HEADS * LANES))],
        out_specs=tile(D_ATTN),
        out_shape=jax.ShapeDtypeStruct((batch * seq_len, D_ATTN), F32),
        scratch_shapes=[pltpu.VMEM((nrow, 2 * LANES), BF16)] + [pltpu.VMEM((nrow, LANES), F32)] * 5
                       + [pltpu.VMEM((N_KV_HEADS * Q_BLOCK, n_c), F32), pltpu.VMEM((ROW_BLOCK, SEL_TK), F32)],
        compiler_params=_cparams(("parallel", "arbitrary")),
        name="attn_prompt",
    )(q, gn, selk, selv, wink, winv, kcvc, csel, _q_placement())


SAMPLE_PAGES_PER_STEP = 32
SAMPLE_TK = 512


def _attn_sample_kernel(pt_ref, q_ref, gn_ref, kcvc_ref, csel_ref, pq_ref, oh_ref, selk_new_ref, selv_new_ref,
                        wink_ref, winv_ref, pages_ref, o_ref,
                        buf_ref, sem_ref, q1_ref, bias_ref, oc_ref, m_ref, l_ref, acc_ref, ow_ref,
                        *, n_groups, gp, past_len, ls):
    b = pl.program_id(0)
    j = pl.program_id(1)
    nrow = N_Q_HEADS * ls
    tk = gp * 128
    blocks_per_step = tk // SEL_BLOCK

    def copies(jj, slot):
        return [pltpu.make_async_copy(pages_ref.at[pt_ref[b, jj * gp + p], pl.ds(2, 2)],
                                      buf_ref.at[slot, p], sem_ref.at[slot])
                for p in range(gp)]

    @pl.when(j == 0)
    def _():
        for c in copies(0, 0):
            c.start()

    slot = j % 2

    @pl.when(j + 1 < n_groups)
    def _():
        for c in copies(j + 1, 1 - slot):
            c.start()

    tok = lax.broadcasted_iota(jnp.int32, (nrow, 1), 0) % ls
    qpos = past_len + tok

    @pl.when(j == 0)
    def _():
        qall = _dot(q_ref[0], pq_ref[...]).astype(BF16)
        q1 = jnp.concatenate([qall[:, c * LANES:(c + 1) * LANES] for c in range(N_Q_HEADS)], axis=0)
        q1_ref[...] = q1
        kc = kcvc_ref[0, :, 0:LANES]
        vc = kcvc_ref[0, :, LANES:2 * LANES]
        n_c = kc.shape[0]
        cmp_end = lax.broadcasted_iota(jnp.int32, (1, n_c), 1) * CMP_STRIDE + (CMP_BLOCK - 1)
        p_c = _softmax2_rows(_nt(q1, kc), cmp_end <= qpos)
        oc_ref[...] = _dot(p_c.astype(BF16), vc)
        hrows = GROUP * ls
        psum = jnp.concatenate(
            [sum(p_c[h * hrows + g * ls:h * hrows + (g + 1) * ls] for g in range(GROUP))
             for h in range(N_KV_HEADS)], axis=0)
        imp_t = _split_dot(psum, csel_ref[...]).T
        blk_t = lax.broadcasted_iota(jnp.int32, imp_t.shape, 0)
        qpos_t = past_len + lax.broadcasted_iota(jnp.int32, imp_t.shape, 1) % ls
        forced_t = jnp.logical_or(blk_t == 0, blk_t == qpos_t // SEL_BLOCK)
        sel_t = _topk_mask_t(imp_t, forced_t, blk_t * SEL_BLOCK <= qpos_t, blk_t)
        bias = jnp.where(sel_t > 0.0, 0.0, NEG).T.astype(BF16)
        for h in range(N_KV_HEADS):
            for g in range(GROUP):
                r0 = (h * GROUP + g) * ls
                bias_ref[r0:r0 + ls, :] = bias[h * ls:(h + 1) * ls]
        kpos_new = past_len + lax.broadcasted_iota(jnp.int32, (1, ls), 1)
        s = jnp.where(kpos_new <= qpos, _nt(q1, selk_new_ref[0]), NEG)
        m0 = jnp.max(s, axis=-1, keepdims=True)
        p = jnp.exp2(s - m0)
        m_ref[...] = jnp.broadcast_to(m0, m_ref.shape)
        l_ref[...] = jnp.broadcast_to(jnp.sum(p, axis=-1, keepdims=True), l_ref.shape)
        acc_ref[...] = _dot(p.astype(BF16), selv_new_ref[0])
        n_w = wink_ref.shape[1]
        kpos_w = past_len + ls - n_w + lax.broadcasted_iota(jnp.int32, (1, n_w), 1)
        dist = qpos - kpos_w
        mask_w = jnp.logical_and(jnp.logical_and(kpos_w >= 0, dist >= 0), dist < WINDOW)
        p_w = _softmax2_rows(_nt(q1, wink_ref[0]), mask_w)
        ow_ref[...] = _dot(p_w.astype(BF16), winv_ref[0])

    for c in copies(j, slot):
        c.wait()

    b0 = pl.multiple_of((j * blocks_per_step // LANES) * LANES, LANES)
    q2 = jnp.concatenate([bias_ref[:, pl.ds(b0, LANES)], q1_ref[...]], axis=1)
    ppt = SAMPLE_TK // 128

    def tile_t(t, kv):
        return jnp.concatenate([buf_ref[slot, t * ppt + p, kv].reshape(LANES, 128) for p in range(ppt)],
                               axis=1).astype(BF16)

    for t in range(gp // ppt):
        k2t = jnp.concatenate([oh_ref[:, t * SAMPLE_TK:(t + 1) * SAMPLE_TK], tile_t(t, 0)], axis=0)
        _flash_rows(m_ref, l_ref, acc_ref, slice(0, nrow), _dot(q2, k2t), tile_t(t, 1), sub=nrow,
                    v_transposed=True)

    @pl.when(j == n_groups - 1)
    def _():
        gn = gn_ref[0]
        for c in range(N_Q_HEADS):
            h = c // GROUP
            rows = slice(c * ls, (c + 1) * ls)
            o_s = acc_ref[rows, :] / jnp.maximum(l_ref[rows, :], 1e-30)
            o = (gn[:, 3 * c:3 * c + 1] * oc_ref[rows, :] + gn[:, 3 * c + 1:3 * c + 2] * o_s
                 + gn[:, 3 * c + 2:3 * c + 3] * ow_ref[rows, :])
            o_ref[0, :, c * HEAD_DIM:(c + 1) * HEAD_DIM] = o[:, h * HEAD_DIM:(h + 1) * HEAD_DIM]


def _attn_sample(page_table, q, gn, kcvc, selk_new, selv_new, wink, winv, pages, past_len, ls):
    bs, n_pages = page_table.shape
    gp = min(SAMPLE_PAGES_PER_STEP, n_pages)
    n_groups = n_pages // gp
    tk = gp * 128
    blocks_per_step = tk // SEL_BLOCK
    assert LANES % blocks_per_step == 0 and tk % SAMPLE_TK == 0
    n_c = kcvc.shape[1]
    n_sel = -(-(past_len + ls) // SEL_BLOCK)
    n_sel_pad = -(-n_sel // LANES) * LANES
    csel = jnp.asarray(_cmp_to_sel(n_c, n_c - 1, n_sel_pad), BF16)
    key_blk = (np.arange(tk) // SEL_BLOCK)[None, :]
    onehot = [jnp.asarray((key_blk + s * blocks_per_step) % LANES == np.arange(LANES)[:, None], BF16)
              for s in range(LANES // blocks_per_step)]
    onehot = jnp.stack(onehot)
    n_rep = LANES // blocks_per_step
    nrow = N_Q_HEADS * ls
    kern = functools.partial(_attn_sample_kernel, n_groups=n_groups, gp=gp, past_len=past_len, ls=ls)
    per_seq = lambda *shape: pl.BlockSpec((1,) + shape, lambda b, j, pt: (b,) + (0,) * len(shape))
    const = lambda shape: pl.BlockSpec(shape, lambda b, j, pt: (0,) * len(shape))
    return pl.pallas_call(
        kern,
        grid_spec=pltpu.PrefetchScalarGridSpec(
            num_scalar_prefetch=1,
            grid=(bs, n_groups),
            in_specs=[per_seq(ls, D_ATTN), per_seq(ls, LANES), per_seq(n_c, 256),
                      const(csel.shape), const((D_ATTN, N_Q_HEADS * LANES)),
                      pl.BlockSpec((None, LANES, tk), lambda b, j, pt: (j % n_rep, 0, 0)),
                      per_seq(ls, LANES), per_seq(ls, LANES),
                      per_seq(wink.shape[1], LANES), per_seq(winv.shape[1], LANES),
                      pl.BlockSpec(memory_space=pl.ANY)],
            out_specs=per_seq(ls, D_ATTN),
            scratch_shapes=[pltpu.VMEM((2, gp, 2, N_KV_HEADS, HEAD_DIM, 128), F32),
                            pltpu.SemaphoreType.DMA((2,)),
                            pltpu.VMEM((nrow, LANES), BF16),
                            pltpu.VMEM((nrow, n_sel_pad), BF16),
                            pltpu.VMEM((nrow, LANES), F32),
                            pltpu.VMEM((nrow, LANES), F32),
                            pltpu.VMEM((nrow, LANES), F32),
                            pltpu.VMEM((nrow, LANES), F32),
                            pltpu.VMEM((nrow, LANES), F32)]),
        out_shape=jax.ShapeDtypeStruct((bs, ls, D_ATTN), F32),
        compiler_params=_cparams(("arbitrary", "arbitrary")),
        name="attn_sample",
    )(page_table, q, gn, kcvc, csel, _q_placement(), onehot, selk_new, selv_new, wink, winv, pages)


def _cmul(ar, ai, br, bi):
    return ar * br - ai * bi, ar * bi + ai * br


def _ssm_weights(lam_re, lam_im, log_dt, b_re, b_im, c_re, c_im, tc):
    dt = jnp.exp(log_dt)[:, None]
    mag = jnp.exp(lam_re * dt)
    a_re = mag * jnp.cos(lam_im * dt)
    a_im = mag * jnp.sin(lam_im * dt)
    den = lam_re * lam_re + lam_im * lam_im
    f_re = ((a_re - 1.0) * lam_re + a_im * lam_im) / den
    f_im = (a_im * lam_re - (a_re - 1.0) * lam_im) / den
    bb_re = f_re[..., None] * b_re - f_im[..., None] * b_im
    bb_im = f_re[..., None] * b_im + f_im[..., None] * b_re
    pr, pi = jnp.ones_like(a_re)[None], jnp.zeros_like(a_re)[None]
    sr, si = a_re, a_im
    while pr.shape[0] < tc + 1:
        nr, ni = _cmul(pr, pi, sr[None], si[None])
        pr, pi = jnp.concatenate([pr, nr]), jnp.concatenate([pi, ni])
        sr, si = _cmul(sr, si, sr, si)
    pr, pi = pr[:tc + 1], pi[:tc + 1]
    cpr = c_re[None] * pr[:, :, None, :] - c_im[None] * pi[:, :, None, :]
    cpi = c_re[None] * pi[:, :, None, :] + c_im[None] * pr[:, :, None, :]
    kker = jnp.einsum('tgop,gpi->gtoi', cpr[:tc], bb_re) - jnp.einsum('tgop,gpi->gtoi', cpi[:tc], bb_im)
    lag = np.arange(tc)[None, :] - np.arange(tc)[:, None]
    kpad = jnp.concatenate([kker, jnp.zeros_like(kker[:, :1])], axis=1)
    toep = kpad[:, np.where(lag >= 0, lag, tc)]
    g = toep.shape[0]
    nset = g // SSM_SET
    eye = jnp.eye(SSM_SET, dtype=F32)
    t6 = jnp.transpose(toep, (0, 1, 4, 2, 3)).reshape(nset, SSM_SET, tc, SSM_GROUP, tc, SSM_GROUP)
    toep_s = jnp.einsum('jgsitc,gh->jsgithc', t6, eye).reshape(nset, tc * LANES, tc * LANES)
    qr, qi = pr[:tc][::-1], pi[:tc][::-1]
    n_re = qr[..., None] * bb_re[None] - qi[..., None] * bb_im[None]
    n_im = qr[..., None] * bb_im[None] + qi[..., None] * bb_re[None]
    n6 = jnp.transpose(jnp.stack([n_re, n_im], axis=2), (1, 0, 4, 2, 3))
    n6 = n6.reshape(nset, SSM_SET, tc, SSM_GROUP, 2, SSM_STATE)
    n_s = jnp.einsum('jgsirp,gh->jsgirhp', n6, eye).reshape(nset, tc * LANES, 2 * SSM_SET * SSM_STATE)
    m6 = jnp.transpose(jnp.stack([cpr[1:tc + 1], -cpi[1:tc + 1]], axis=3), (1, 3, 4, 0, 2))
    m6 = m6.reshape(nset, SSM_SET, 2, SSM_STATE, tc, SSM_GROUP)
    m_s = jnp.einsum('jgrptc,gh->jrgpthc', m6, eye).reshape(nset, 2 * SSM_SET * SSM_STATE, tc * LANES)
    half = SSM_SET * SSM_STATE
    return (toep_s.astype(BF16), n_s.astype(BF16), m_s.astype(BF16),
            pr[tc].reshape(1, nset * half), pi[tc].reshape(1, nset * half))


def _chunk_rows(u_ref, tc):
    n = u_ref.shape[0] // tc
    return jnp.concatenate([u_ref[pl.ds(s, n, stride=tc), :] for s in range(tc)], axis=1)


def _ssm_state_kernel(u_ref, n_ref, s_ref, *, tc):
    s_ref[...] = _dot(_chunk_rows(u_ref, tc).astype(BF16), n_ref[0])


def _ssm_carry_kernel(s_ref, h0_ref, are_ref, aim_ref, hs_ref, f_ref, h_scr, *, batch, nset):
    half = SSM_SET * SSM_STATE

    @pl.when(pl.program_id(0) == 0)
    def _():
        h_scr[...] = h0_ref[...]

    def body(k, h):
        sk = jnp.concatenate([s_ref[b, pl.ds(k, 1), :] for b in range(batch)], axis=0)
        for b in range(batch):
            hs_ref[b, pl.ds(k, 1), :] = h[b:b + 1]
        out = []
        for j in range(nset):
            c0 = 2 * j * half
            re, im = h[:, c0:c0 + half], h[:, c0 + half:c0 + 2 * half]
            ar, ai = are_ref[:, j * half:(j + 1) * half], aim_ref[:, j * half:(j + 1) * half]
            out.append(ar * re - ai * im + sk[:, c0:c0 + half])
            out.append(ar * im + ai * re + sk[:, c0 + half:c0 + 2 * half])
        return jnp.concatenate(out, axis=1)

    h = lax.fori_loop(0, s_ref.shape[1], body, h_scr[...])
    h_scr[...] = h
    f_ref[...] = h


def _ssm_out_kernel(u_ref, t_ref, m_ref, h_ref, y_ref, *, tc):
    y = _dot(_chunk_rows(u_ref, tc).astype(BF16), t_ref[0]) + _dot(h_ref[...].astype(BF16), m_ref[0])
    n = y.shape[0]
    for s in range(tc):
        y_ref[pl.ds(s, n, stride=tc), :] = y[:, s * LANES:(s + 1) * LANES]


def _ssm(u2d, batch, seq_len, tc, h0_re, h0_im, ssm_w):
    toep, nmat, mmat, at_re, at_im = ssm_w
    nset = toep.shape[0]
    half = SSM_SET * SSM_STATE
    ncol = 2 * nset * half
    t_tok = batch * seq_len
    nck = seq_len // tc
    tm = min(2048, t_tok)
    rt = tm // tc
    grid = (nset, t_tok // tm)
    u_spec = pl.BlockSpec((tm, LANES), lambda j, i: (i, j))
    w_spec = lambda w: pl.BlockSpec((1,) + w.shape[1:], lambda j, i: (j, 0, 0))
    st_spec = pl.BlockSpec((rt, 2 * half), lambda j, i: (i, j))
    s = pl.pallas_call(
        functools.partial(_ssm_state_kernel, tc=tc),
        grid=grid,
        in_specs=[u_spec, w_spec(nmat)],
        out_specs=st_spec,
        out_shape=jax.ShapeDtypeStruct((batch * nck, ncol), F32),
        compiler_params=_cparams(("parallel", "parallel")),
        name="ssm_state",
    )(u2d, nmat)
    kb = min(128, nck)
    h0 = jnp.stack([h0_re.reshape(batch, nset, half), h0_im.reshape(batch, nset, half)], axis=2).reshape(batch, ncol)
    seq3 = pl.BlockSpec((batch, kb, ncol), lambda c: (0, c, 0))
    hs, f = pl.pallas_call(
        functools.partial(_ssm_carry_kernel, batch=batch, nset=nset),
        grid=(nck // kb,),
        in_specs=[seq3, _full((batch, ncol)), _full(at_re.shape), _full(at_im.shape)],
        out_specs=[seq3, _full((batch, ncol))],
        out_shape=[jax.ShapeDtypeStruct((batch, nck, ncol), F32), jax.ShapeDtypeStruct((batch, ncol), F32)],
        scratch_shapes=[pltpu.VMEM((batch, ncol), F32)],
        compiler_params=_cparams(("arbitrary",)),
        name="ssm_carry",
    )(s.reshape(batch, nck, ncol), h0, at_re, at_im)
    y = pl.pallas_call(
        functools.partial(_ssm_out_kernel, tc=tc),
        grid=grid,
        in_specs=[u_spec, w_spec(toep), w_spec(mmat), st_spec],
        out_specs=u_spec,
        out_shape=jax.ShapeDtypeStruct(u2d.shape, F32),
        compiler_params=_cparams(("parallel", "parallel")),
        name="ssm_out",
    )(u2d, toep, mmat, hs.reshape(batch * nck, ncol))
    f = f.reshape(batch, nset, 2, SSM_SET, SSM_STATE)
    g = nset * SSM_SET
    return y, f[:, :, 0].reshape(batch, g, SSM_STATE), f[:, :, 1].reshape(batch, g, SSM_STATE)


def _post_kernel(x_ref, oa_ref, ys_ref, u_ref, g_ref, fg_ref, d_ref, bglu_ref,
                 wza_ref, wzs_ref, wgm_ref, wglu_ref, wla_ref, wls_ref, wo_ref, out_ref):
    x = x_ref[...]
    d = x.shape[1]
    xn = x * lax.rsqrt(jnp.mean(x * x, axis=-1, keepdims=True) + RMS_EPS) * g_ref[...]
    xb = xn.astype(BF16)
    z_a = _dot(xb, wza_ref[...])
    z_s = _dot(xb, wzs_ref[...])
    gm = jax.nn.sigmoid(_dot(xb, wgm_ref[...]))
    branch_a = _dot((oa_ref[...] * jax.nn.silu(z_a)).astype(BF16), wla_ref[...])
    y = jax.nn.gelu(ys_ref[...] + d_ref[...] * u_ref[...])
    y = y * jax.nn.sigmoid(_dot(y.astype(BF16), wglu_ref[...]) + bglu_ref[...])
    branch_b = _dot((y * jax.nn.silu(z_s)).astype(BF16), wls_ref[...])
    merged = gm[:, :d] * branch_a + gm[:, d:] * branch_b
    r = x + _dot(merged.astype(BF16), wo_ref[...])
    out_ref[...] = r * lax.rsqrt(jnp.mean(r * r, axis=-1, keepdims=True) + RMS_EPS) * fg_ref[...]


def _post(x2d, o_attn, y_ssm, u, norm_g, final_g, ssm_d, b_glu, wza, wzs, wgm, wglu, wla, wls, wo):
    t, d = x2d.shape
    tm = min(256, t)
    row = lambda n: pl.BlockSpec((tm, n), lambda i: (i, 0))
    ws = [wza, wzs, wgm, wglu, wla, wls, wo]
    vecs = [norm_g.reshape(1, d), final_g.reshape(1, d), ssm_d.reshape(1, -1), b_glu.reshape(1, -1)]
    return pl.pallas_call(
        _post_kernel,
        grid=(t // tm,),
        in_specs=[row(d), row(o_attn.shape[1]), row(y_ssm.shape[1]), row(u.shape[1])]
                 + [_full(v.shape) for v in vecs] + [_full(w.shape) for w in ws],
        out_specs=row(d),
        out_shape=jax.ShapeDtypeStruct((t, d), F32),
        compiler_params=_cparams(("parallel",)),
        name="post",
    )(x2d, o_attn, y_ssm, u, *vecs, *ws)


SSM_CHUNK = 8


def kernel(x_prompt, x_sample, cache_kv, cache_win_kv, state_ssm_re, state_ssm_im, page_table, norm_g, w_in, cmp_pe, cmp_w1, cmp_b1, cmp_w2, ssm_lam_re, ssm_lam_im, ssm_log_dt, ssm_b_re, ssm_b_im, ssm_c_re, ssm_c_im, ssm_d, w_glu, b_glu, w_lift_attn, w_lift_ssm, w_out, final_g):
    depth = norm_g.shape[0]
    assert depth == 1
    l = 0
    bp, lp, d = x_prompt.shape
    bs, ls, _ = x_sample.shape
    n_pages, page = page_table.shape[1], cache_kv.shape[2]
    past_len = n_pages * page
    w_buf = cache_win_kv.shape[2]
    d_ssm = ssm_d.shape[1]
    n_grp = d_ssm // SSM_GROUP
    assert page == 128 and w_buf == WINDOW and lp % Q_BLOCK == 0 and lp >= WINDOW

    splits = (D_ATTN, N_KV_SLOTS * N_KV_HEADS * HEAD_DIM, 3 * N_Q_HEADS, D_ATTN, d_ssm, d_ssm, 2 * d)
    offs = np.concatenate([[0], np.cumsum(splits)])
    wb = w_in[l].astype(BF16)
    wq, wkv, wg, wza, wu, wzs, wgm = [wb[:, offs[k]:offs[k + 1]] for k in range(7)]
    wg = jnp.pad(wg, ((0, 0), (0, LANES - wg.shape[1])))
    cw1, cb, cw2 = _compress_weights(cmp_pe[l], cmp_w1[l], cmp_b1[l], cmp_w2[l])
    ssm_p = (ssm_lam_re[l], ssm_lam_im[l], ssm_log_dt[l], ssm_b_re[l], ssm_b_im[l], ssm_c_re[l], ssm_c_im[l])
    post_w = (norm_g[l], final_g, ssm_d[l], b_glu[l], wza, wzs, wgm, w_glu[l].astype(BF16),
              w_lift_attn[l].astype(BF16), w_lift_ssm[l].astype(BF16), w_out[l].astype(BF16))

    xp = x_prompt.reshape(bp * lp, d)
    q, kv_t, win_t, cmp_rows, selk, selv, wink, winv, gn, u = _pre(xp, norm_g[l], wq, wkv, wg, wu, lp, True)
    ident = jnp.arange(bp * (lp // 128), dtype=jnp.int32).reshape(bp, lp // 128)
    kcvc = _compress(ident, cmp_rows.reshape(bp * lp // 128, 128, 2 * LANES), cw1, cb, cw2)
    o_attn = _attn_prompt(q, gn, selk, selv, wink, winv, kcvc, bp, lp)
    assert lp % SSM_CHUNK == 0 and ls == SSM_CHUNK
    ssm_w = _ssm_weights(*ssm_p, SSM_CHUNK)
    h0 = jnp.zeros((bp, n_grp, SSM_STATE), F32)
    y_ssm, hr_p, hi_p = _ssm(u, bp, lp, SSM_CHUNK, h0, h0, ssm_w)
    y_prompt = _post(xp, o_attn, y_ssm, u, *post_w).reshape(bp, lp, d)
    kv_prompt = jnp.transpose(kv_t.reshape(bp, 4, N_KV_HEADS, HEAD_DIM, lp), (0, 4, 1, 2, 3))[None]
    win_t = win_t.reshape(bp, 2, N_KV_HEADS, HEAD_DIM, lp)[..., lp - WINDOW:]
    win_prompt = jnp.transpose(win_t, (0, 4, 1, 2, 3))[None]

    xs = x_sample.reshape(bs * ls, d)
    q, kv, win, _, selk, selv, wink, winv, gn, u = _pre(xs, norm_g[l], wq, wkv, wg, wu, ls, False)
    pages = jnp.transpose(cache_kv[l], (0, 2, 3, 4, 1))
    kcvc = _compress(page_table, pages, cw1, cb, cw2)
    cwin = cache_win_kv[l].reshape(bs, w_buf, 2 * LANES)
    wk_all = jnp.concatenate([cwin[:, :, :LANES].astype(BF16), wink.reshape(bs, ls, LANES)], axis=1)
    wv_all = jnp.concatenate([cwin[:, :, LANES:].astype(BF16), winv.reshape(bs, ls, LANES)], axis=1)
    o_attn = _attn_sample(page_table, q.reshape(bs, ls, D_ATTN), gn.reshape(bs, ls, LANES), kcvc,
                          selk[:, :LANES].reshape(bs, ls, LANES), selv.reshape(bs, ls, LANES),
                          wk_all, wv_all, pages, past_len, ls)
    y_ssm, hr_s, hi_s = _ssm(u, bs, ls, SSM_CHUNK, state_ssm_re[l], state_ssm_im[l], ssm_w)
    y_sample = _post(xs, o_attn.reshape(bs * ls, D_ATTN), y_ssm, u, *post_w).reshape(bs, ls, d)
    kv_sample = kv.reshape(1, bs, ls, 4, N_KV_HEADS, HEAD_DIM)
    win_new = win.reshape(bs, ls, 2, N_KV_HEADS, HEAD_DIM)
    win_sample = jnp.concatenate([cache_win_kv[l], win_new], axis=1)[None, :, ls:]

    return (y_prompt, y_sample, kv_prompt, win_prompt, hr_p[None], hi_p[None],
            kv_sample, win_sample, hr_s[None], hi_s[None])
```

```python
import functools
import math

import numpy as np
import jax
import jax.numpy as jnp
from jax import lax
from jax.experimental import pallas as pl
from jax.experimental.pallas import tpu as pltpu

F32 = jnp.float32
BF16 = jnp.bfloat16

HEAD_DIM = 64
N_KV_HEADS = 2
GROUP = 4
N_Q_HEADS = N_KV_HEADS * GROUP
D_ATTN = N_Q_HEADS * HEAD_DIM
N_KV_SLOTS = 6
CMP_BLOCK = 32
CMP_STRIDE = 16
SEL_BLOCK = 64
N_SEL = 16
WINDOW = 512
Q_BLOCK = 128
SSM_GROUP = 16
SSM_STATE = 64
SSM_SET = 8
RMS_EPS = 1e-6
LANES = 128
KV_ROW = 4 * N_KV_HEADS * HEAD_DIM
NEG = -1e30
LOG2E = math.log2(math.e)
VMEM_LIMIT = 56 * 1024 * 1024


def _cparams(sem):
    return pltpu.CompilerParams(dimension_semantics=sem, vmem_limit_bytes=VMEM_LIMIT)


def _full(shape):
    nd = len(shape)
    return pl.BlockSpec(shape, lambda *_: (0,) * nd)


def _nt(a, b):
    return lax.dot_general(a, b, (((1,), (1,)), ((), ())), preferred_element_type=F32)


def _dot(a, b):
    return jnp.dot(a, b, preferred_element_type=F32)


def _pre_kernel(x_ref, g_ref, wq_ref, wkv_ref, wg_ref, wu_ref,
                q_ref, kv_ref, win_ref, cmp_ref, selk_ref, selv_ref, wink_ref, winv_ref, gn_ref, u_ref,
                *, seq_len, tm, transposed_kv):
    x = x_ref[...]
    xn = x * lax.rsqrt(jnp.mean(x * x, axis=-1, keepdims=True) + RMS_EPS) * g_ref[...]
    xb = xn.astype(BF16)
    q = _dot(xb, wq_ref[...])
    q_ref[...] = (q * (LOG2E * HEAD_DIM ** -0.5)).astype(BF16)
    kv = _dot(xb, wkv_ref[...])
    if transposed_kv:
        for c in range(4):
            kv_ref[c * LANES:(c + 1) * LANES, :] = kv[:, c * LANES:(c + 1) * LANES].T
        for c in range(2):
            win_ref[c * LANES:(c + 1) * LANES, :] = kv[:, KV_ROW + c * LANES:KV_ROW + (c + 1) * LANES].T
    else:
        kv_ref[...] = kv[:, :KV_ROW]
        win_ref[...] = kv[:, KV_ROW:]
    cmp_ref[...] = kv[:, 0:2 * LANES]
    pos = (pl.program_id(0) * tm + lax.broadcasted_iota(jnp.int32, (tm, LANES), 0)) % seq_len
    onehot = lax.broadcasted_iota(jnp.int32, (tm, LANES), 1) == (pos // SEL_BLOCK) % LANES
    selk_ref[:, :LANES] = kv[:, 256:384].astype(BF16)
    selk_ref[:, LANES:] = jnp.where(onehot, 1.0, 0.0).astype(BF16)
    selv_ref[...] = kv[:, 384:512].astype(BF16)
    wink_ref[...] = kv[:, 512:640].astype(BF16)
    winv_ref[...] = kv[:, 640:768].astype(BF16)
    gn_ref[...] = jax.nn.sigmoid(_dot(xb, wg_ref[...]))
    u_ref[...] = _dot(xb, wu_ref[...])


def _pre(x2d, norm_g, wq, wkv, wg, wu, seq_len, transposed_kv):
    t, d = x2d.shape
    tm = min(256, t)
    kern = functools.partial(_pre_kernel, seq_len=seq_len, tm=tm, transposed_kv=transposed_kv)
    row = lambda n: pl.BlockSpec((tm, n), lambda i: (i, 0))
    outs = [(D_ATTN, BF16), (KV_ROW, F32), (2 * LANES, F32), (2 * LANES, F32), (2 * LANES, BF16), (LANES, BF16),
            (LANES, BF16), (LANES, BF16), (LANES, F32), (wu.shape[1], F32)]
    out_specs = [row(n) for n, _ in outs]
    out_shape = [jax.ShapeDtypeStruct((t, n), dt) for n, dt in outs]
    if transposed_kv:
        per_seq = seq_len // tm
        for k in (1, 2):
            n = outs[k][0]
            out_specs[k] = pl.BlockSpec((None, n, tm), lambda i: (i // per_seq, 0, i % per_seq))
            out_shape[k] = jax.ShapeDtypeStruct((t // seq_len, n, seq_len), F32)
    return pl.pallas_call(
        kern,
        grid=(t // tm,),
        in_specs=[row(d), _full((1, d)), _full(wq.shape), _full(wkv.shape), _full(wg.shape), _full(wu.shape)],
        out_specs=out_specs,
        out_shape=out_shape,
        compiler_params=_cparams(("parallel",)),
        name="pre",
    )(x2d, norm_g.reshape(1, d), wq, wkv, wg, wu)


PAGES_PER_STEP = 32
CMP_PAGE_PARTS = 2


def _compress_kernel(pt_ref, pages_ref, w1_ref, b_ref, w2_ref, out_ref, buf_ref, sem_ref, h_ref, *tbuf_ref,
                     n_groups, gp, transposed):
    b = pl.program_id(0)
    j = pl.program_id(1)
    rows = gp * (128 // CMP_STRIDE)

    def copies(jj, slot):
        if transposed:
            return [pltpu.make_async_copy(pages_ref.at[pt_ref[b, jj * gp + p], pl.ds(0, 2)],
                                          tbuf_ref[0].at[slot, p], sem_ref.at[slot])
                    for p in range(gp)]
        return [pltpu.make_async_copy(pages_ref.at[pt_ref[b, jj * gp + p], :, pl.ds(half * LANES, LANES)],
                                      buf_ref.at[slot, half, p], sem_ref.at[slot])
                for p in range(gp) for half in range(2)]

    @pl.when(j == 0)
    def _():
        for c in copies(0, 0):
            c.start()

    slot = j % 2

    @pl.when(j + 1 < n_groups)
    def _():
        for c in copies(j + 1, 1 - slot):
            c.start()

    for c in copies(j, slot):
        c.wait()

    n_part = CMP_PAGE_PARTS if transposed and gp % CMP_PAGE_PARTS == 0 else 1
    pp = gp // n_part
    prows = rows // n_part
    rslot = 0 if transposed else slot
    for part in range(n_part):
        p0 = part * pp
        if transposed:
            for p in range(p0, p0 + pp):
                for half in range(2):
                    buf_ref[0, half, p] = tbuf_ref[0][slot, p, half].reshape(LANES, 128).T
        acc = jnp.zeros((prows, 512), F32)
        for jj in range(CMP_STRIDE):
            xj = [buf_ref[rslot, half, p0:p0 + pp, pl.ds(jj, 128 // CMP_STRIDE, stride=CMP_STRIDE), :]
                  .reshape(prows, LANES) for half in range(2)]
            xj = jnp.concatenate(xj, axis=1).astype(BF16)
            acc = acc + _dot(xj, w1_ref[jj])
        h_ref[pl.ds(pl.multiple_of(j * rows + part * prows, prows), prows), :] = acc

    @pl.when(j == n_groups - 1)
    def _():
        n = h_ref.shape[0]
        hb = pltpu.roll(h_ref[:, 256:512], n - 1, 0)
        hid = h_ref[:, 0:256] + hb + b_ref[...]
        act = jax.nn.gelu(hid).astype(BF16)
        out_ref[0] = _dot(act, w2_ref[...]).astype(BF16)


def _compress(page_table, pages, w1, b_eff, w2):
    n_seq, n_pages = page_table.shape
    gp = min(PAGES_PER_STEP, n_pages)
    n_groups = n_pages // gp
    n_chunks = n_pages * (128 // CMP_STRIDE)
    transposed = pages.ndim == 5
    kern = functools.partial(_compress_kernel, n_groups=n_groups, gp=gp, transposed=transposed)
    scratch = [pltpu.VMEM((1 if transposed else 2, 2, gp, 128, LANES), F32),
               pltpu.SemaphoreType.DMA((2,)),
               pltpu.VMEM((n_chunks, 512), F32)]
    if transposed:
        scratch.append(pltpu.VMEM((2, gp, 2, N_KV_HEADS, HEAD_DIM, 128), F32))
    return pl.pallas_call(
        kern,
        grid_spec=pltpu.PrefetchScalarGridSpec(
            num_scalar_prefetch=1,
            grid=(n_seq, n_groups),
            in_specs=[pl.BlockSpec(memory_space=pl.ANY),
                      pl.BlockSpec(w1.shape, lambda b, j, pt: (0, 0, 0)),
                      pl.BlockSpec(b_eff.shape, lambda b, j, pt: (0, 0)),
                      pl.BlockSpec(w2.shape, lambda b, j, pt: (0, 0))],
            out_specs=pl.BlockSpec((1, n_chunks, 256), lambda b, j, pt: (b, 0, 0)),
            scratch_shapes=scratch),
        out_shape=jax.ShapeDtypeStruct((n_seq, n_chunks, 256), BF16),
        compiler_params=_cparams(("arbitrary", "arbitrary")),
        name="compress",
    )(page_table, pages, w1, b_eff, w2)


def _compress_weights(cmp_pe, cmp_w1, cmp_b1, cmp_w2):
    w1 = cmp_w1.reshape(2, 2, CMP_STRIDE, HEAD_DIM, HEAD_DIM)
    eye = jnp.eye(N_KV_HEADS, dtype=F32)
    blk = jnp.einsum('vrjde,vw,hg->rjvhdwge', w1, jnp.eye(2, dtype=F32), eye)
    blk = blk.reshape(2, CMP_STRIDE, 256, 256)
    w1_all = jnp.concatenate([blk[0], blk[1]], axis=-1).astype(BF16)
    pe = cmp_pe.reshape(2, 2, CMP_STRIDE, HEAD_DIM)
    pe_bias = jnp.einsum('vrjd,vrjde->ve', pe, w1) + cmp_b1
    b_eff = jnp.broadcast_to(pe_bias[:, None, :], (2, N_KV_HEADS, HEAD_DIM)).reshape(1, 256)
    w2 = jnp.einsum('ved,vw,hg->vhewgd', cmp_w2, jnp.eye(2, dtype=F32), eye).reshape(256, 256).astype(BF16)
    return w1_all, b_eff, w2


def _cmp_to_sel(n_cmp_rows, n_valid, n_sel_pad):
    c0 = np.arange(n_cmp_rows)[:, None] * CMP_STRIDE
    s0 = np.arange(n_sel_pad)[None, :] * SEL_BLOCK
    shared = np.minimum(c0 + CMP_BLOCK, s0 + SEL_BLOCK) - np.maximum(c0, s0)
    m = np.clip(shared, 0, None).astype(np.float32) / CMP_BLOCK
    m[n_valid:] = 0.0
    return m


def _softmax2_rows(s, mask):
    s = jnp.where(mask, s, -jnp.inf)
    m = jnp.max(s, axis=-1, keepdims=True)
    m = jnp.where(jnp.isfinite(m), m, 0.0)
    e = jnp.exp2(s - m)
    return e / jnp.maximum(jnp.sum(e, axis=-1, keepdims=True), 1e-30)


def _split_dot(p, w):
    hi = p.astype(BF16)
    lo = (p - hi.astype(F32)).astype(BF16)
    return _dot(hi, w) + _dot(lo, w)


def _topk_mask_t(imp_t, forced_t, causal_t, blk_t):
    v = jnp.where(forced_t, jnp.inf, jnp.where(causal_t, imp_t, -jnp.inf))
    big = jnp.int32(imp_t.shape[0])
    sel = jnp.zeros(imp_t.shape, F32)
    for _ in range(N_SEL):
        mx = jnp.max(v, axis=0, keepdims=True)
        idx = jnp.min(jnp.where(v == mx, blk_t, big), axis=0, keepdims=True)
        pick = blk_t == idx
        sel = jnp.where(pick, jnp.maximum(sel, jnp.where(mx > -jnp.inf, 1.0, 0.0)), sel)
        v = jnp.where(pick, -jnp.inf, v)
    return sel


def _lane_groups(s):
    return [s[:, j * LANES:(j + 1) * LANES] for j in range(s.shape[1] // LANES)]


def _one_ahead(items, make):
    nxt = make(items[0])
    for n, item in enumerate(items):
        cur = nxt
        if n + 1 < len(items):
            nxt = make(items[n + 1])
        yield item, cur


def _flash_rows(m_ref, l_ref, acc_ref, rows, s, v, mask_fn=None, sub=64, v_transposed=False):
    r = s.shape[0]
    ps, alphas = [], []
    for u in range(r // sub):
        rr = slice(rows.start + u * sub, rows.start + (u + 1) * sub)
        su = s[u * sub:(u + 1) * sub]
        if mask_fn is not None:
            su = jnp.where(mask_fn(rr), su, NEG)
        sj = _lane_groups(su)
        m_prev = m_ref[rr, :]
        m_new = jnp.maximum(m_prev, jnp.max(functools.reduce(jnp.maximum, sj), axis=-1, keepdims=True))
        alpha = jnp.exp2(m_prev - m_new)
        pj = [jnp.exp2(x - m_new) for x in sj]
        l_ref[rr, :] = alpha * l_ref[rr, :] + jnp.sum(functools.reduce(jnp.add, pj), axis=-1, keepdims=True)
        m_ref[rr, :] = m_new
        ps.append(jnp.concatenate([x.astype(BF16) for x in pj], axis=1))
        alphas.append(alpha)
    p = jnp.concatenate(ps, axis=0)
    pv = _nt(p, v) if v_transposed else _dot(p, v)
    acc_ref[rows, :] = jnp.concatenate(alphas, axis=0) * acc_ref[rows, :] + pv


def _softmax_weights(s, mask_fn, row0, sub=64):
    out = []
    for u in range(s.shape[0] // sub):
        su = jnp.where(mask_fn(slice(row0 + u * sub, row0 + (u + 1) * sub)), s[u * sub:(u + 1) * sub], -jnp.inf)
        sj = _lane_groups(su)
        m = jnp.max(functools.reduce(jnp.maximum, sj), axis=-1, keepdims=True)
        m = jnp.where(jnp.isfinite(m), m, 0.0)
        ej = [jnp.exp2(x - m) for x in sj]
        inv = 1.0 / jnp.maximum(jnp.sum(functools.reduce(jnp.add, ej), axis=-1, keepdims=True), 1e-30)
        out.append(jnp.concatenate([e * inv for e in ej], axis=1))
    return jnp.concatenate(out, axis=0)


def _softmax_pv_rows(s, v, mask_fn, row0, sub=64):
    r = s.shape[0]
    ps, ls = [], []
    for u in range(r // sub):
        su = jnp.where(mask_fn(slice(row0 + u * sub, row0 + (u + 1) * sub)), s[u * sub:(u + 1) * sub], NEG)
        sj = _lane_groups(su)
        m = jnp.max(functools.reduce(jnp.maximum, sj), axis=-1, keepdims=True)
        pj = [jnp.exp2(x - m) for x in sj]
        ls.append(jnp.broadcast_to(jnp.sum(functools.reduce(jnp.add, pj), axis=-1, keepdims=True), (sub, LANES)))
        ps.append(jnp.concatenate([x.astype(BF16) for x in pj], axis=1))
    return _dot(jnp.concatenate(ps, axis=0), v) / jnp.maximum(jnp.concatenate(ls, axis=0), 1e-30)


SEL_TK = 512
SEL_SUB = 64
ROW_BLOCK = 512
WIN_ROW_BLOCK = 512
WIN_KEYS = WINDOW + Q_BLOCK


def _attn_prompt_kernel(q_ref, qn_ref, gn_ref, selk_ref, selv_ref, wink_ref, winv_ref, kcvc_ref, csel_ref, pq_ref,
                        o_ref, q2_ref, m_ref, l_ref, acc_ref, oc_ref, ow_ref, ps_ref, s0_ref, *, nq):
    i = pl.program_id(1)
    nrow = N_Q_HEADS * Q_BLOCK
    slot = i % 2
    blocks = [slice(r * ROW_BLOCK, (r + 1) * ROW_BLOCK) for r in range(nrow // ROW_BLOCK)]
    heads = [slice(c * Q_BLOCK, (c + 1) * Q_BLOCK) for c in range(N_Q_HEADS)]

    def rel(rr, n, step=1):
        shape = (rr.stop - rr.start, n)
        return lax.broadcasted_iota(jnp.int32, shape, 1) * step - lax.broadcasted_iota(jnp.int32, shape, 0)

    def qpos0(iq, rr):
        return iq * Q_BLOCK + rr.start % Q_BLOCK

    def prepare_cmp(qt_ref, iq, sl):
        qall = _dot(qt_ref[...], pq_ref[...]).astype(BF16)
        for c in range(N_Q_HEADS):
            q2_ref[sl, c * Q_BLOCK:(c + 1) * Q_BLOCK, 0:LANES] = qall[:, c * LANES:(c + 1) * LANES]
        kc = kcvc_ref[0, :, 0:LANES]
        vc = kcvc_ref[0, :, LANES:2 * LANES]
        n_c = kc.shape[0]
        for rows, s in _one_ahead(heads, lambda rr: _nt(q2_ref[sl, rr, 0:LANES], kc)):
            c = rows.start // Q_BLOCK
            p_c = _softmax_weights(
                s, lambda rr: rel(rr, n_c, CMP_STRIDE) <= qpos0(iq, rr) - (CMP_BLOCK - 1), rows.start)
            oc_ref[sl, rows, :] = _dot(p_c.astype(BF16), vc)
            prow = slice((c // GROUP) * Q_BLOCK, (c // GROUP + 1) * Q_BLOCK)
            if c % GROUP == 0:
                ps_ref[prow, :] = p_c
            else:
                ps_ref[prow, :] += p_c

    def prepare_sel(iq, sl):
        imp_t = _split_dot(ps_ref[...], csel_ref[...]).T
        blk_t = lax.broadcasted_iota(jnp.int32, imp_t.shape, 0)
        qpos_t = iq * Q_BLOCK + lax.broadcasted_iota(jnp.int32, imp_t.shape, 1) % Q_BLOCK
        forced_t = jnp.logical_or(blk_t == 0, blk_t == qpos_t // SEL_BLOCK)
        sel_t = _topk_mask_t(imp_t, forced_t, blk_t * SEL_BLOCK <= qpos_t, blk_t)
        bias = jnp.where(sel_t > 0.0, 0.0, NEG).T.astype(BF16)
        for c in range(N_Q_HEADS):
            h = c // GROUP
            q2_ref[sl, c * Q_BLOCK:(c + 1) * Q_BLOCK, LANES:2 * LANES] = bias[h * Q_BLOCK:(h + 1) * Q_BLOCK]

    @pl.when(i == 0)
    def _():
        prepare_cmp(q_ref, 0, 0)
        prepare_sel(0, 0)

    m_ref[...] = jnp.full(m_ref.shape, -jnp.inf, F32)
    l_ref[...] = jnp.zeros(l_ref.shape, F32)
    acc_ref[...] = jnp.zeros(acc_ref.shape, F32)

    def scores(rr, t):
        return _nt(q2_ref[slot, rr, :], selk_ref[pl.ds(pl.multiple_of(t * SEL_TK, SEL_TK), SEL_TK), :])

    def sel_step(t, diagonal):
        k0 = pl.multiple_of(t * SEL_TK, SEL_TK)
        v2 = selv_ref[pl.ds(k0, SEL_TK), :]
        mask_fn = (lambda rr: rel(rr, SEL_TK) <= qpos0(i, rr) - k0) if diagonal else None
        pending = s0_ref[...]
        for n, rows in enumerate(blocks):
            s = pending
            if n + 1 < len(blocks):
                pending = scores(blocks[n + 1], t)
            elif not diagonal:
                s0_ref[...] = scores(blocks[0], t + 1)
            _flash_rows(m_ref, l_ref, acc_ref, rows, s, v2, mask_fn, sub=SEL_SUB)

    n_below = (i * Q_BLOCK) // SEL_TK
    s0_ref[...] = scores(blocks[0], 0)

    def body(t2, carry):
        sel_step(2 * t2, False)
        sel_step(2 * t2 + 1, False)
        return carry

    lax.fori_loop(0, n_below // 2, body, 0)

    @pl.when(n_below % 2 == 1)
    def _():
        sel_step(n_below - 1, False)

    i_next = jnp.minimum(i + 1, nq - 1)
    sel_step(n_below, True)
    prepare_cmp(qn_ref, i_next, 1 - slot)
    prepare_sel(i_next, 1 - slot)

    w0 = pl.multiple_of(jnp.maximum(i - WINDOW // Q_BLOCK, 0) * Q_BLOCK, Q_BLOCK)
    kw = wink_ref[pl.ds(w0, WIN_KEYS), :]
    vw = winv_ref[pl.ds(w0, WIN_KEYS), :]

    def win_mask(rr):
        dist = (qpos0(i, rr) - w0) - rel(rr, WIN_KEYS)
        return pltpu.bitcast(dist, jnp.uint32) < WINDOW

    wblocks = [slice(r * WIN_ROW_BLOCK, (r + 1) * WIN_ROW_BLOCK) for r in range(nrow // WIN_ROW_BLOCK)]
    for rows, s in _one_ahead(wblocks, lambda rr: _nt(q2_ref[slot, rr, 0:LANES], kw)):
        ow_ref[rows, :] = _softmax_pv_rows(s, vw, win_mask, rows.start)

    gn = gn_ref[...]
    for c in range(N_Q_HEADS):
        h = c // GROUP
        rows = heads[c]
        o_s = acc_ref[rows, :] / jnp.maximum(l_ref[rows, :], 1e-30)
        o = (gn[:, 3 * c:3 * c + 1] * oc_ref[slot, rows, :] + gn[:, 3 * c + 1:3 * c + 2] * o_s
             + gn[:, 3 * c + 2:3 * c + 3] * ow_ref[rows, :])
        o_ref[:, c * HEAD_DIM:(c + 1) * HEAD_DIM] = o[:, h * HEAD_DIM:(h + 1) * HEAD_DIM]


def _q_placement():
    p = np.zeros((D_ATTN, N_Q_HEADS * LANES), np.float32)
    for c in range(N_Q_HEADS):
        h = c // GROUP
        for d in range(HEAD_DIM):
            p[c * HEAD_DIM + d, c * LANES + h * HEAD_DIM + d] = 1.0
    return jnp.asarray(p, BF16)


def _attn_prompt(q, gn, selk, selv, wink, winv, kcvc, batch, seq_len):
    nq = seq_len // Q_BLOCK
    n_c = kcvc.shape[1]
    assert seq_len % SEL_TK == 0 and seq_len // SEL_BLOCK <= LANES and seq_len >= WIN_KEYS and n_c % LANES == 0
    csel = jnp.asarray(_cmp_to_sel(n_c, n_c - 1, LANES), BF16)
    nrow = N_Q_HEADS * Q_BLOCK
    tile = lambda n: pl.BlockSpec((Q_BLOCK, n), lambda b, i: (b * nq + i, 0))
    next_tile = pl.BlockSpec((Q_BLOCK, D_ATTN), lambda b, i: (b * nq + jnp.minimum(i + 1, nq - 1), 0))
    seq = lambda n: pl.BlockSpec((seq_len, n), lambda b, i: (b, 0))
    rows_f32 = pltpu.VMEM((nrow, LANES), F32)
    return pl.pallas_call(
        functools.partial(_attn_prompt_kernel, nq=nq),
        grid=(batch, nq),
        in_specs=[tile(D_ATTN), next_tile, tile(LANES), seq(2 * LANES), seq(LANES), seq(LANES), seq(LANES),
                  pl.BlockSpec((1, n_c, 256), lambda b, i: (b, 0, 0)),
                  _full(csel.shape), _full((D_ATTN, N_Q_HEADS * LANES))],
        out_specs=tile(D_ATTN),
        out_shape=jax.ShapeDtypeStruct((batch * seq_len, D_ATTN), F32),
        scratch_shapes=[pltpu.VMEM((2, nrow, 2 * LANES), BF16), rows_f32, rows_f32, rows_f32,
                        pltpu.VMEM((2, nrow, LANES), F32), rows_f32,
                        pltpu.VMEM((N_KV_HEADS * Q_BLOCK, n_c), F32), pltpu.VMEM((ROW_BLOCK, SEL_TK), F32)],
        compiler_params=_cparams(("parallel", "arbitrary")),
        name="attn_prompt",
    )(q, q, gn, selk, selv, wink, winv, kcvc, csel, _q_placement())


SAMPLE_PAGES_PER_STEP = 32
SAMPLE_TK = 512


def _attn_sample_kernel(pt_ref, q_ref, gn_ref, kcvc_ref, csel_ref, pq_ref, oh_ref, selk_new_ref, selv_new_ref,
                        wink_ref, winv_ref, pages_ref, o_ref,
                        buf_ref, sem_ref, q1_ref, bias_ref, oc_ref, m_ref, l_ref, acc_ref, ow_ref,
                        *, n_groups, gp, past_len, ls):
    b = pl.program_id(0)
    j = pl.program_id(1)
    nrow = N_Q_HEADS * ls
    tk = gp * 128
    blocks_per_step = tk // SEL_BLOCK

    def copies(jj, slot):
        return [pltpu.make_async_copy(pages_ref.at[pt_ref[b, jj * gp + p], pl.ds(2, 2)],
                                      buf_ref.at[slot, p], sem_ref.at[slot])
                for p in range(gp)]

    @pl.when(j == 0)
    def _():
        for c in copies(0, 0):
            c.start()

    slot = j % 2

    @pl.when(j + 1 < n_groups)
    def _():
        for c in copies(j + 1, 1 - slot):
            c.start()

    tok = lax.broadcasted_iota(jnp.int32, (nrow, 1), 0) % ls
    qpos = past_len + tok

    @pl.when(j == 0)
    def _():
        qall = _dot(q_ref[0], pq_ref[...]).astype(BF16)
        q1 = jnp.concatenate([qall[:, c * LANES:(c + 1) * LANES] for c in range(N_Q_HEADS)], axis=0)
        q1_ref[...] = q1
        kc = kcvc_ref[0, :, 0:LANES]
        vc = kcvc_ref[0, :, LANES:2 * LANES]
        n_c = kc.shape[0]
        cmp_end = lax.broadcasted_iota(jnp.int32, (1, n_c), 1) * CMP_STRIDE + (CMP_BLOCK - 1)
        p_c = _softmax2_rows(_nt(q1, kc), cmp_end <= qpos)
        oc_ref[...] = _dot(p_c.astype(BF16), vc)
        hrows = GROUP * ls
        psum = jnp.concatenate(
            [sum(p_c[h * hrows + g * ls:h * hrows + (g + 1) * ls] for g in range(GROUP))
             for h in range(N_KV_HEADS)], axis=0)
        imp_t = _split_dot(psum, csel_ref[...]).T
        blk_t = lax.broadcasted_iota(jnp.int32, imp_t.shape, 0)
        qpos_t = past_len + lax.broadcasted_iota(jnp.int32, imp_t.shape, 1) % ls
        forced_t = jnp.logical_or(blk_t == 0, blk_t == qpos_t // SEL_BLOCK)
        sel_t = _topk_mask_t(imp_t, forced_t, blk_t * SEL_BLOCK <= qpos_t, blk_t)
        bias = jnp.where(sel_t > 0.0, 0.0, NEG).T.astype(BF16)
        for h in range(N_KV_HEADS):
            for g in range(GROUP):
                r0 = (h * GROUP + g) * ls
                bias_ref[r0:r0 + ls, :] = bias[h * ls:(h + 1) * ls]
        kpos_new = past_len + lax.broadcasted_iota(jnp.int32, (1, ls), 1)
        s = jnp.where(kpos_new <= qpos, _nt(q1, selk_new_ref[0]), NEG)
        m0 = jnp.max(s, axis=-1, keepdims=True)
        p = jnp.exp2(s - m0)
        m_ref[...] = jnp.broadcast_to(m0, m_ref.shape)
        l_ref[...] = jnp.broadcast_to(jnp.sum(p, axis=-1, keepdims=True), l_ref.shape)
        acc_ref[...] = _dot(p.astype(BF16), selv_new_ref[0])
        n_w = wink_ref.shape[1]
        kpos_w = past_len + ls - n_w + lax.broadcasted_iota(jnp.int32, (1, n_w), 1)
        dist = qpos - kpos_w
        mask_w = jnp.logical_and(jnp.logical_and(kpos_w >= 0, dist >= 0), dist < WINDOW)
        p_w = _softmax2_rows(_nt(q1, wink_ref[0]), mask_w)
        ow_ref[...] = _dot(p_w.astype(BF16), winv_ref[0])

    for c in copies(j, slot):
        c.wait()

    b0 = pl.multiple_of((j * blocks_per_step // LANES) * LANES, LANES)
    q2 = jnp.concatenate([bias_ref[:, pl.ds(b0, LANES)], q1_ref[...]], axis=1)
    ppt = SAMPLE_TK // 128

    def tile_t(t, kv):
        return jnp.concatenate([buf_ref[slot, t * ppt + p, kv].reshape(LANES, 128) for p in range(ppt)],
                               axis=1).astype(BF16)

    def scores(t):
        return _dot(q2, jnp.concatenate([oh_ref[:, t * SAMPLE_TK:(t + 1) * SAMPLE_TK], tile_t(t, 0)], axis=0))

    for t, s in _one_ahead(list(range(gp // ppt)), scores):
        _flash_rows(m_ref, l_ref, acc_ref, slice(0, nrow), s, tile_t(t, 1), sub=nrow, v_transposed=True)

    @pl.when(j == n_groups - 1)
    def _():
        gn = gn_ref[0]
        for c in range(N_Q_HEADS):
            h = c // GROUP
            rows = slice(c * ls, (c + 1) * ls)
            o_s = acc_ref[rows, :] / jnp.maximum(l_ref[rows, :], 1e-30)
            o = (gn[:, 3 * c:3 * c + 1] * oc_ref[rows, :] + gn[:, 3 * c + 1:3 * c + 2] * o_s
                 + gn[:, 3 * c + 2:3 * c + 3] * ow_ref[rows, :])
            o_ref[0, :, c * HEAD_DIM:(c + 1) * HEAD_DIM] = o[:, h * HEAD_DIM:(h + 1) * HEAD_DIM]


def _attn_sample(page_table, q, gn, kcvc, selk_new, selv_new, wink, winv, pages, past_len, ls):
    bs, n_pages = page_table.shape
    gp = min(SAMPLE_PAGES_PER_STEP, n_pages)
    n_groups = n_pages // gp
    tk = gp * 128
    blocks_per_step = tk // SEL_BLOCK
    assert LANES % blocks_per_step == 0 and tk % SAMPLE_TK == 0
    n_c = kcvc.shape[1]
    n_sel = -(-(past_len + ls) // SEL_BLOCK)
    n_sel_pad = -(-n_sel // LANES) * LANES
    csel = jnp.asarray(_cmp_to_sel(n_c, n_c - 1, n_sel_pad), BF16)
    key_blk = (np.arange(tk) // SEL_BLOCK)[None, :]
    onehot = [jnp.asarray((key_blk + s * blocks_per_step) % LANES == np.arange(LANES)[:, None], BF16)
              for s in range(LANES // blocks_per_step)]
    onehot = jnp.stack(onehot)
    n_rep = LANES // blocks_per_step
    nrow = N_Q_HEADS * ls
    kern = functools.partial(_attn_sample_kernel, n_groups=n_groups, gp=gp, past_len=past_len, ls=ls)
    per_seq = lambda *shape: pl.BlockSpec((1,) + shape, lambda b, j, pt: (b,) + (0,) * len(shape))
    const = lambda shape: pl.BlockSpec(shape, lambda b, j, pt: (0,) * len(shape))
    return pl.pallas_call(
        kern,
        grid_spec=pltpu.PrefetchScalarGridSpec(
            num_scalar_prefetch=1,
            grid=(bs, n_groups),
            in_specs=[per_seq(ls, D_ATTN), per_seq(ls, LANES), per_seq(n_c, 256),
                      const(csel.shape), const((D_ATTN, N_Q_HEADS * LANES)),
                      pl.BlockSpec((None, LANES, tk), lambda b, j, pt: (j % n_rep, 0, 0)),
                      per_seq(ls, LANES), per_seq(ls, LANES),
                      per_seq(wink.shape[1], LANES), per_seq(winv.shape[1], LANES),
                      pl.BlockSpec(memory_space=pl.ANY)],
            out_specs=per_seq(ls, D_ATTN),
            scratch_shapes=[pltpu.VMEM((2, gp, 2, N_KV_HEADS, HEAD_DIM, 128), F32),
                            pltpu.SemaphoreType.DMA((2,)),
                            pltpu.VMEM((nrow, LANES), BF16),
                            pltpu.VMEM((nrow, n_sel_pad), BF16),
                            pltpu.VMEM((nrow, LANES), F32),
                            pltpu.VMEM((nrow, LANES), F32),
                            pltpu.VMEM((nrow, LANES), F32),
                            pltpu.VMEM((nrow, LANES), F32),
                            pltpu.VMEM((nrow, LANES), F32)]),
        out_shape=jax.ShapeDtypeStruct((bs, ls, D_ATTN), F32),
        compiler_params=_cparams(("arbitrary", "arbitrary")),
        name="attn_sample",
    )(page_table, q, gn, kcvc, csel, _q_placement(), onehot, selk_new, selv_new, wink, winv, pages)


def _cmul(ar, ai, br, bi):
    return ar * br - ai * bi, ar * bi + ai * br


def _ssm_weights(lam_re, lam_im, log_dt, b_re, b_im, c_re, c_im, tc):
    dt = jnp.exp(log_dt)[:, None]
    mag = jnp.exp(lam_re * dt)
    a_re = mag * jnp.cos(lam_im * dt)
    a_im = mag * jnp.sin(lam_im * dt)
    den = lam_re * lam_re + lam_im * lam_im
    f_re = ((a_re - 1.0) * lam_re + a_im * lam_im) / den
    f_im = (a_im * lam_re - (a_re - 1.0) * lam_im) / den
    bb_re = f_re[..., None] * b_re - f_im[..., None] * b_im
    bb_im = f_re[..., None] * b_im + f_im[..., None] * b_re
    pr, pi = jnp.ones_like(a_re)[None], jnp.zeros_like(a_re)[None]
    sr, si = a_re, a_im
    while pr.shape[0] < tc + 1:
        nr, ni = _cmul(pr, pi, sr[None], si[None])
        pr, pi = jnp.concatenate([pr, nr]), jnp.concatenate([pi, ni])
        sr, si = _cmul(sr, si, sr, si)
    pr, pi = pr[:tc + 1], pi[:tc + 1]
    cpr = c_re[None] * pr[:, :, None, :] - c_im[None] * pi[:, :, None, :]
    cpi = c_re[None] * pi[:, :, None, :] + c_im[None] * pr[:, :, None, :]
    kker = jnp.einsum('tgop,gpi->gtoi', cpr[:tc], bb_re) - jnp.einsum('tgop,gpi->gtoi', cpi[:tc], bb_im)
    g = a_re.shape[0]
    nset = g // SSM_SET
    half = SSM_SET * SSM_STATE

    def expand(table, row_sizes, col_sizes, index_fn):
        tab = table.reshape(nset, -1).astype(BF16)
        tab = jnp.concatenate([tab, jnp.zeros((nset, 1), BF16)], axis=1)
        r = np.unravel_index(np.arange(int(np.prod(row_sizes))), row_sizes)
        c = np.unravel_index(np.arange(int(np.prod(col_sizes))), col_sizes)
        idx, valid = index_fn([x[:, None] for x in r], [x[None, :] for x in c])
        return tab[:, np.where(valid, idx, tab.shape[1] - 1).astype(np.int32)]

    sg, sp, sc = SSM_SET, SSM_STATE, SSM_GROUP
    toep_s = expand(kker, (tc, sg, sc), (tc, sg, sc),
                    lambda r, c: ((((r[1] * tc + (c[0] - r[0])) * sc + c[2]) * sc + r[2]),
                                  (r[1] == c[1]) & (c[0] >= r[0])))
    qr, qi = pr[:tc][::-1], pi[:tc][::-1]
    n_re = qr[..., None] * bb_re[None] - qi[..., None] * bb_im[None]
    n_im = qr[..., None] * bb_im[None] + qi[..., None] * bb_re[None]
    n_tab = jnp.transpose(jnp.stack([n_re, n_im], axis=2), (1, 0, 4, 2, 3))
    n_s = expand(n_tab, (tc, sg, sc), (2, sg, sp),
                 lambda r, c: (((((r[1] * tc + r[0]) * sc + r[2]) * 2 + c[0]) * sp + c[2]), (r[1] == c[1]) & (c[0] >= 0)))
    m_tab = jnp.transpose(jnp.stack([cpr[1:tc + 1], -cpi[1:tc + 1]], axis=3), (1, 3, 4, 0, 2))
    m_s = expand(m_tab, (2, sg, sp), (tc, sg, sc),
                 lambda r, c: (((((r[1] * 2 + r[0]) * sp + r[2]) * tc + c[0]) * sc + c[2]), (r[1] == c[1]) & (c[0] >= 0)))
    return toep_s, n_s, m_s, pr[tc].reshape(1, nset * half), pi[tc].reshape(1, nset * half)


def _chunk_rows(u_ref, tc):
    n = u_ref.shape[0] // tc
    return jnp.concatenate([u_ref[pl.ds(s, n, stride=tc), :] for s in range(tc)], axis=1)


def _ssm_state_kernel(u_ref, n_ref, s_ref, *, tc):
    s_ref[...] = _dot(_chunk_rows(u_ref, tc).astype(BF16), n_ref[0])


def _ssm_carry_kernel(s_ref, h0_ref, are_ref, aim_ref, hs_ref, f_ref, h_scr, *, batch, nset):
    half = SSM_SET * SSM_STATE

    @pl.when(pl.program_id(0) == 0)
    def _():
        h_scr[...] = h0_ref[...]

    def body(k, h):
        sk = jnp.concatenate([s_ref[b, pl.ds(k, 1), :] for b in range(batch)], axis=0)
        for b in range(batch):
            hs_ref[b, pl.ds(k, 1), :] = h[b:b + 1]
        out = []
        for j in range(nset):
            c0 = 2 * j * half
            re, im = h[:, c0:c0 + half], h[:, c0 + half:c0 + 2 * half]
            ar, ai = are_ref[:, j * half:(j + 1) * half], aim_ref[:, j * half:(j + 1) * half]
            out.append(ar * re - ai * im + sk[:, c0:c0 + half])
            out.append(ar * im + ai * re + sk[:, c0 + half:c0 + 2 * half])
        return jnp.concatenate(out, axis=1)

    h = lax.fori_loop(0, s_ref.shape[1], body, h_scr[...])
    h_scr[...] = h
    f_ref[...] = h


def _ssm_out_kernel(u_ref, t_ref, m_ref, h_ref, y_ref, *, tc):
    y = _dot(_chunk_rows(u_ref, tc).astype(BF16), t_ref[0]) + _dot(h_ref[...].astype(BF16), m_ref[0])
    n = y.shape[0]
    for s in range(tc):
        y_ref[pl.ds(s, n, stride=tc), :] = y[:, s * LANES:(s + 1) * LANES]


def _ssm(u2d, batch, seq_len, tc, h0_re, h0_im, ssm_w):
    toep, nmat, mmat, at_re, at_im = ssm_w
    nset = toep.shape[0]
    half = SSM_SET * SSM_STATE
    ncol = 2 * nset * half
    t_tok = batch * seq_len
    nck = seq_len // tc
    tm = min(2048, t_tok)
    rt = tm // tc
    grid = (nset, t_tok // tm)
    u_spec = pl.BlockSpec((tm, LANES), lambda j, i: (i, j))
    w_spec = lambda w: pl.BlockSpec((1,) + w.shape[1:], lambda j, i: (j, 0, 0))
    st_spec = pl.BlockSpec((rt, 2 * half), lambda j, i: (i, j))
    s = pl.pallas_call(
        functools.partial(_ssm_state_kernel, tc=tc),
        grid=grid,
        in_specs=[u_spec, w_spec(nmat)],
        out_specs=st_spec,
        out_shape=jax.ShapeDtypeStruct((batch * nck, ncol), F32),
        compiler_params=_cparams(("parallel", "parallel")),
        name="ssm_state",
    )(u2d, nmat)
    kb = min(128, nck)
    h0 = jnp.stack([h0_re.reshape(batch, nset, half), h0_im.reshape(batch, nset, half)], axis=2).reshape(batch, ncol)
    seq3 = pl.BlockSpec((batch, kb, ncol), lambda c: (0, c, 0))
    hs, f = pl.pallas_call(
        functools.partial(_ssm_carry_kernel, batch=batch, nset=nset),
        grid=(nck // kb,),
        in_specs=[seq3, _full((batch, ncol)), _full(at_re.shape), _full(at_im.shape)],
        out_specs=[seq3, _full((batch, ncol))],
        out_shape=[jax.ShapeDtypeStruct((batch, nck, ncol), F32), jax.ShapeDtypeStruct((batch, ncol), F32)],
        scratch_shapes=[pltpu.VMEM((batch, ncol), F32)],
        compiler_params=_cparams(("arbitrary",)),
        name="ssm_carry",
    )(s.reshape(batch, nck, ncol), h0, at_re, at_im)
    y = pl.pallas_call(
        functools.partial(_ssm_out_kernel, tc=tc),
        grid=grid,
        in_specs=[u_spec, w_spec(toep), w_spec(mmat), st_spec],
        out_specs=u_spec,
        out_shape=jax.ShapeDtypeStruct(u2d.shape, F32),
        compiler_params=_cparams(("parallel", "parallel")),
        name="ssm_out",
    )(u2d, toep, mmat, hs.reshape(batch * nck, ncol))
    f = f.reshape(batch, nset, 2, SSM_SET, SSM_STATE)
    g = nset * SSM_SET
    return y, f[:, :, 0].reshape(batch, g, SSM_STATE), f[:, :, 1].reshape(batch, g, SSM_STATE)


def _post_kernel(x_ref, oa_ref, ys_ref, u_ref, g_ref, fg_ref, d_ref, bglu_ref,
                 wza_ref, wzs_ref, wgm_ref, wglu_ref, wla_ref, wls_ref, wo_ref, out_ref):
    x = x_ref[...]
    d = x.shape[1]
    xn = x * lax.rsqrt(jnp.mean(x * x, axis=-1, keepdims=True) + RMS_EPS) * g_ref[...]
    xb = xn.astype(BF16)
    z_a = _dot(xb, wza_ref[...])
    z_s = _dot(xb, wzs_ref[...])
    gm = jax.nn.sigmoid(_dot(xb, wgm_ref[...]))
    branch_a = _dot((oa_ref[...] * jax.nn.silu(z_a)).astype(BF16), wla_ref[...])
    y = jax.nn.gelu(ys_ref[...] + d_ref[...] * u_ref[...])
    y = y * jax.nn.sigmoid(_dot(y.astype(BF16), wglu_ref[...]) + bglu_ref[...])
    branch_b = _dot((y * jax.nn.silu(z_s)).astype(BF16), wls_ref[...])
    merged = gm[:, :d] * branch_a + gm[:, d:] * branch_b
    r = x + _dot(merged.astype(BF16), wo_ref[...])
    out_ref[...] = r * lax.rsqrt(jnp.mean(r * r, axis=-1, keepdims=True) + RMS_EPS) * fg_ref[...]


def _post(x2d, o_attn, y_ssm, u, norm_g, final_g, ssm_d, b_glu, wza, wzs, wgm, wglu, wla, wls, wo):
    t, d = x2d.shape
    tm = min(256, t)
    row = lambda n: pl.BlockSpec((tm, n), lambda i: (i, 0))
    ws = [wza, wzs, wgm, wglu, wla, wls, wo]
    vecs = [norm_g.reshape(1, d), final_g.reshape(1, d), ssm_d.reshape(1, -1), b_glu.reshape(1, -1)]
    return pl.pallas_call(
        _post_kernel,
        grid=(t // tm,),
        in_specs=[row(d), row(o_attn.shape[1]), row(y_ssm.shape[1]), row(u.shape[1])]
                 + [_full(v.shape) for v in vecs] + [_full(w.shape) for w in ws],
        out_specs=row(d),
        out_shape=jax.ShapeDtypeStruct((t, d), F32),
        compiler_params=_cparams(("parallel",)),
        name="post",
    )(x2d, o_attn, y_ssm, u, *vecs, *ws)


SSM_CHUNK = 8


def kernel(x_prompt, x_sample, cache_kv, cache_win_kv, state_ssm_re, state_ssm_im, page_table, norm_g, w_in, cmp_pe, cmp_w1, cmp_b1, cmp_w2, ssm_lam_re, ssm_lam_im, ssm_log_dt, ssm_b_re, ssm_b_im, ssm_c_re, ssm_c_im, ssm_d, w_glu, b_glu, w_lift_attn, w_lift_ssm, w_out, final_g):
    depth = norm_g.shape[0]
    assert depth == 1
    l = 0
    bp, lp, d = x_prompt.shape
    bs, ls, _ = x_sample.shape
    n_pages, page = page_table.shape[1], cache_kv.shape[2]
    past_len = n_pages * page
    w_buf = cache_win_kv.shape[2]
    d_ssm = ssm_d.shape[1]
    n_grp = d_ssm // SSM_GROUP
    assert page == 128 and w_buf == WINDOW and lp % Q_BLOCK == 0 and lp >= WINDOW

    splits = (D_ATTN, N_KV_SLOTS * N_KV_HEADS * HEAD_DIM, 3 * N_Q_HEADS, D_ATTN, d_ssm, d_ssm, 2 * d)
    offs = np.concatenate([[0], np.cumsum(splits)])
    wb = w_in[l].astype(BF16)
    wq, wkv, wg, wza, wu, wzs, wgm = [wb[:, offs[k]:offs[k + 1]] for k in range(7)]
    wg = jnp.pad(wg, ((0, 0), (0, LANES - wg.shape[1])))
    cw1, cb, cw2 = _compress_weights(cmp_pe[l], cmp_w1[l], cmp_b1[l], cmp_w2[l])
    ssm_p = (ssm_lam_re[l], ssm_lam_im[l], ssm_log_dt[l], ssm_b_re[l], ssm_b_im[l], ssm_c_re[l], ssm_c_im[l])
    post_w = (norm_g[l], final_g, ssm_d[l], b_glu[l], wza, wzs, wgm, w_glu[l].astype(BF16),
              w_lift_attn[l].astype(BF16), w_lift_ssm[l].astype(BF16), w_out[l].astype(BF16))

    xp = x_prompt.reshape(bp * lp, d)
    q, kv_t, win_t, cmp_rows, selk, selv, wink, winv, gn, u = _pre(xp, norm_g[l], wq, wkv, wg, wu, lp, True)
    ident = jnp.arange(bp * (lp // 128), dtype=jnp.int32).reshape(bp, lp // 128)
    kcvc = _compress(ident, cmp_rows.reshape(bp * lp // 128, 128, 2 * LANES), cw1, cb, cw2)
    o_attn = _attn_prompt(q, gn, selk, selv, wink, winv, kcvc, bp, lp)
    assert lp % SSM_CHUNK == 0 and ls == SSM_CHUNK
    ssm_w = _ssm_weights(*ssm_p, SSM_CHUNK)
    h0 = jnp.zeros((bp, n_grp, SSM_STATE), F32)
    y_ssm, hr_p, hi_p = _ssm(u, bp, lp, SSM_CHUNK, h0, h0, ssm_w)
    y_prompt = _post(xp, o_attn, y_ssm, u, *post_w).reshape(bp, lp, d)
    kv_prompt = jnp.transpose(kv_t.reshape(bp, 4, N_KV_HEADS, HEAD_DIM, lp), (0, 4, 1, 2, 3))[None]
    win_t = win_t.reshape(bp, 2, N_KV_HEADS, HEAD_DIM, lp)[..., lp - WINDOW:]
    win_prompt = jnp.transpose(win_t, (0, 4, 1, 2, 3))[None]

    xs = x_sample.reshape(bs * ls, d)
    q, kv, win, _, selk, selv, wink, winv, gn, u = _pre(xs, norm_g[l], wq, wkv, wg, wu, ls, False)
    pages = jnp.transpose(cache_kv[l], (0, 2, 3, 4, 1))
    kcvc = _compress(page_table, pages, cw1, cb, cw2)
    cwin = cache_win_kv[l].reshape(bs, w_buf, 2 * LANES)
    wk_all = jnp.concatenate([cwin[:, :, :LANES].astype(BF16), wink.reshape(bs, ls, LANES)], axis=1)
    wv_all = jnp.concatenate([cwin[:, :, LANES:].astype(BF16), winv.reshape(bs, ls, LANES)], axis=1)
    o_attn = _attn_sample(page_table, q.reshape(bs, ls, D_ATTN), gn.reshape(bs, ls, LANES), kcvc,
                          selk[:, :LANES].reshape(bs, ls, LANES), selv.reshape(bs, ls, LANES),
                          wk_all, wv_all, pages, past_len, ls)
    y_ssm, hr_s, hi_s = _ssm(u, bs, ls, SSM_CHUNK, state_ssm_re[l], state_ssm_im[l], ssm_w)
    y_sample = _post(xs, o_attn.reshape(bs * ls, D_ATTN), y_ssm, u, *post_w).reshape(bs, ls, d)
    kv_sample = kv.reshape(1, bs, ls, 4, N_KV_HEADS, HEAD_DIM)
    win_new = win.reshape(bs, ls, 2, N_KV_HEADS, HEAD_DIM)
    win_sample = jnp.concatenate([cache_win_kv[l], win_new], axis=1)[None, :, ls:]

    return (y_prompt, y_sample, kv_prompt, win_prompt, hr_p[None], hi_p[None],
            kv_sample, win_sample, hr_s[None], hi_s[None])
```

```python
import functools
import math

import numpy as np
import jax
import jax.numpy as jnp
from jax import lax
from jax.experimental import pallas as pl
from jax.experimental.pallas import tpu as pltpu

F32 = jnp.float32
BF16 = jnp.bfloat16

HEAD_DIM = 64
N_KV_HEADS = 2
GROUP = 4
N_Q_HEADS = N_KV_HEADS * GROUP
D_ATTN = N_Q_HEADS * HEAD_DIM
N_KV_SLOTS = 6
CMP_BLOCK = 32
CMP_STRIDE = 16
SEL_BLOCK = 64
N_SEL = 16
WINDOW = 512
Q_BLOCK = 128
SSM_GROUP = 16
SSM_STATE = 64
SSM_SET = 8
RMS_EPS = 1e-6
LANES = 128
KV_ROW = 4 * N_KV_HEADS * HEAD_DIM
NEG = -1e30
LOG2E = math.log2(math.e)
VMEM_LIMIT = 56 * 1024 * 1024


def _cparams(sem):
    return pltpu.CompilerParams(dimension_semantics=sem, vmem_limit_bytes=VMEM_LIMIT)


def _full(shape):
    nd = len(shape)
    return pl.BlockSpec(shape, lambda *_: (0,) * nd)


def _nt(a, b):
    return lax.dot_general(a, b, (((1,), (1,)), ((), ())), preferred_element_type=F32)


def _dot(a, b):
    return jnp.dot(a, b, preferred_element_type=F32)


def _pre_kernel(x_ref, g_ref, wq_ref, wkv_ref, wg_ref, wu_ref,
                q_ref, kv_ref, win_ref, cmp_ref, selk_ref, selv_ref, wink_ref, winv_ref, gn_ref, u_ref,
                *, seq_len, tm, transposed_kv):
    x = x_ref[...]
    xn = x * lax.rsqrt(jnp.mean(x * x, axis=-1, keepdims=True) + RMS_EPS) * g_ref[...]
    xb = xn.astype(BF16)
    q = _dot(xb, wq_ref[...])
    q_ref[...] = (q * (LOG2E * HEAD_DIM ** -0.5)).astype(BF16)
    kv = _dot(xb, wkv_ref[...])
    if transposed_kv:
        for c in range(4):
            kv_ref[c * LANES:(c + 1) * LANES, :] = kv[:, c * LANES:(c + 1) * LANES].T
        for c in range(2):
            win_ref[c * LANES:(c + 1) * LANES, :] = kv[:, KV_ROW + c * LANES:KV_ROW + (c + 1) * LANES].T
    else:
        kv_ref[...] = kv[:, :KV_ROW]
        win_ref[...] = kv[:, KV_ROW:]
    cmp_ref[...] = kv[:, 0:2 * LANES]
    pos = (pl.program_id(0) * tm + lax.broadcasted_iota(jnp.int32, (tm, LANES), 0)) % seq_len
    onehot = lax.broadcasted_iota(jnp.int32, (tm, LANES), 1) == (pos // SEL_BLOCK) % LANES
    selk_ref[:, :LANES] = kv[:, 256:384].astype(BF16)
    selk_ref[:, LANES:] = jnp.where(onehot, 1.0, 0.0).astype(BF16)
    selv_ref[...] = kv[:, 384:512].astype(BF16)
    wink_ref[...] = kv[:, 512:640].astype(BF16)
    winv_ref[...] = kv[:, 640:768].astype(BF16)
    gn_ref[...] = jax.nn.sigmoid(_dot(xb, wg_ref[...]))
    u_ref[...] = _dot(xb, wu_ref[...])


def _pre(x2d, norm_g, wq, wkv, wg, wu, seq_len, transposed_kv):
    t, d = x2d.shape
    tm = min(256, t)
    kern = functools.partial(_pre_kernel, seq_len=seq_len, tm=tm, transposed_kv=transposed_kv)
    row = lambda n: pl.BlockSpec((tm, n), lambda i: (i, 0))
    outs = [(D_ATTN, BF16), (KV_ROW, F32), (2 * LANES, F32), (2 * LANES, F32), (2 * LANES, BF16), (LANES, BF16),
            (LANES, BF16), (LANES, BF16), (LANES, F32), (wu.shape[1], F32)]
    out_specs = [row(n) for n, _ in outs]
    out_shape = [jax.ShapeDtypeStruct((t, n), dt) for n, dt in outs]
    if transposed_kv:
        per_seq = seq_len // tm
        for k in (1, 2):
            n = outs[k][0]
            out_specs[k] = pl.BlockSpec((None, n, tm), lambda i: (i // per_seq, 0, i % per_seq))
            out_shape[k] = jax.ShapeDtypeStruct((t // seq_len, n, seq_len), F32)
    return pl.pallas_call(
        kern,
        grid=(t // tm,),
        in_specs=[row(d), _full((1, d)), _full(wq.shape), _full(wkv.shape), _full(wg.shape), _full(wu.shape)],
        out_specs=out_specs,
        out_shape=out_shape,
        compiler_params=_cparams(("parallel",)),
        name="pre",
    )(x2d, norm_g.reshape(1, d), wq, wkv, wg, wu)


PAGES_PER_STEP = 32
CMP_PAGE_PARTS = 2


def _compress_kernel(pt_ref, pages_ref, w1_ref, b_ref, w2_ref, out_ref, buf_ref, sem_ref, h_ref, *tbuf_ref,
                     n_groups, gp, transposed):
    b = pl.program_id(0)
    j = pl.program_id(1)
    rows = gp * (128 // CMP_STRIDE)

    def copies(jj, slot):
        if transposed:
            return [pltpu.make_async_copy(pages_ref.at[pt_ref[b, jj * gp + p], pl.ds(0, 2)],
                                          tbuf_ref[0].at[slot, p], sem_ref.at[slot])
                    for p in range(gp)]
        return [pltpu.make_async_copy(pages_ref.at[pt_ref[b, jj * gp + p], :, pl.ds(half * LANES, LANES)],
                                      buf_ref.at[slot, half, p], sem_ref.at[slot])
                for p in range(gp) for half in range(2)]

    @pl.when(j == 0)
    def _():
        for c in copies(0, 0):
            c.start()

    slot = j % 2

    @pl.when(j + 1 < n_groups)
    def _():
        for c in copies(j + 1, 1 - slot):
            c.start()

    for c in copies(j, slot):
        c.wait()

    n_part = CMP_PAGE_PARTS if transposed and gp % CMP_PAGE_PARTS == 0 else 1
    pp = gp // n_part
    prows = rows // n_part
    rslot = 0 if transposed else slot
    for part in range(n_part):
        p0 = part * pp
        if transposed:
            for p in range(p0, p0 + pp):
                for half in range(2):
                    buf_ref[0, half, p] = tbuf_ref[0][slot, p, half].reshape(LANES, 128).T
        acc = jnp.zeros((prows, 512), F32)
        for jj in range(CMP_STRIDE):
            xj = [buf_ref[rslot, half, p0:p0 + pp, pl.ds(jj, 128 // CMP_STRIDE, stride=CMP_STRIDE), :]
                  .reshape(prows, LANES) for half in range(2)]
            xj = jnp.concatenate(xj, axis=1).astype(BF16)
            acc = acc + _dot(xj, w1_ref[jj])
        h_ref[pl.ds(pl.multiple_of(j * rows + part * prows, prows), prows), :] = acc

    @pl.when(j == n_groups - 1)
    def _():
        n = h_ref.shape[0]
        hb = pltpu.roll(h_ref[:, 256:512], n - 1, 0)
        hid = h_ref[:, 0:256] + hb + b_ref[...]
        act = jax.nn.gelu(hid).astype(BF16)
        out_ref[0] = _dot(act, w2_ref[...]).astype(BF16)


def _compress(page_table, pages, w1, b_eff, w2):
    n_seq, n_pages = page_table.shape
    gp = min(PAGES_PER_STEP, n_pages)
    n_groups = n_pages // gp
    n_chunks = n_pages * (128 // CMP_STRIDE)
    transposed = pages.ndim == 5
    kern = functools.partial(_compress_kernel, n_groups=n_groups, gp=gp, transposed=transposed)
    scratch = [pltpu.VMEM((1 if transposed else 2, 2, gp, 128, LANES), F32),
               pltpu.SemaphoreType.DMA((2,)),
               pltpu.VMEM((n_chunks, 512), F32)]
    if transposed:
        scratch.append(pltpu.VMEM((2, gp, 2, N_KV_HEADS, HEAD_DIM, 128), F32))
    return pl.pallas_call(
        kern,
        grid_spec=pltpu.PrefetchScalarGridSpec(
            num_scalar_prefetch=1,
            grid=(n_seq, n_groups),
            in_specs=[pl.BlockSpec(memory_space=pl.ANY),
                      pl.BlockSpec(w1.shape, lambda b, j, pt: (0, 0, 0)),
                      pl.BlockSpec(b_eff.shape, lambda b, j, pt: (0, 0)),
                      pl.BlockSpec(w2.shape, lambda b, j, pt: (0, 0))],
            out_specs=pl.BlockSpec((1, n_chunks, 256), lambda b, j, pt: (b, 0, 0)),
            scratch_shapes=scratch),
        out_shape=jax.ShapeDtypeStruct((n_seq, n_chunks, 256), BF16),
        compiler_params=_cparams(("arbitrary", "arbitrary")),
        name="compress",
    )(page_table, pages, w1, b_eff, w2)


def _compress_weights(cmp_pe, cmp_w1, cmp_b1, cmp_w2):
    w1 = cmp_w1.reshape(2, 2, CMP_STRIDE, HEAD_DIM, HEAD_DIM)
    eye = jnp.eye(N_KV_HEADS, dtype=F32)
    blk = jnp.einsum('vrjde,vw,hg->rjvhdwge', w1, jnp.eye(2, dtype=F32), eye)
    blk = blk.reshape(2, CMP_STRIDE, 256, 256)
    w1_all = jnp.concatenate([blk[0], blk[1]], axis=-1).astype(BF16)
    pe = cmp_pe.reshape(2, 2, CMP_STRIDE, HEAD_DIM)
    pe_bias = jnp.einsum('vrjd,vrjde->ve', pe, w1) + cmp_b1
    b_eff = jnp.broadcast_to(pe_bias[:, None, :], (2, N_KV_HEADS, HEAD_DIM)).reshape(1, 256)
    w2 = jnp.einsum('ved,vw,hg->vhewgd', cmp_w2, jnp.eye(2, dtype=F32), eye).reshape(256, 256).astype(BF16)
    return w1_all, b_eff, w2


def _cmp_to_sel(n_cmp_rows, n_valid, n_sel_pad):
    c0 = np.arange(n_cmp_rows)[:, None] * CMP_STRIDE
    s0 = np.arange(n_sel_pad)[None, :] * SEL_BLOCK
    shared = np.minimum(c0 + CMP_BLOCK, s0 + SEL_BLOCK) - np.maximum(c0, s0)
    m = np.clip(shared, 0, None).astype(np.float32) / CMP_BLOCK
    m[n_valid:] = 0.0
    return m


def _softmax2_rows(s, mask):
    s = jnp.where(mask, s, -jnp.inf)
    m = jnp.max(s, axis=-1, keepdims=True)
    m = jnp.where(jnp.isfinite(m), m, 0.0)
    e = jnp.exp2(s - m)
    return e / jnp.maximum(jnp.sum(e, axis=-1, keepdims=True), 1e-30)


def _split_dot(p, w):
    hi = p.astype(BF16)
    lo = (p - hi.astype(F32)).astype(BF16)
    return _dot(hi, w) + _dot(lo, w)


def _topk_mask_t(imp_t, forced_t, causal_t, blk_t):
    v = jnp.where(forced_t, jnp.inf, jnp.where(causal_t, imp_t, -jnp.inf))
    big = jnp.int32(imp_t.shape[0])
    sel = jnp.zeros(imp_t.shape, F32)
    for _ in range(N_SEL):
        mx = jnp.max(v, axis=0, keepdims=True)
        idx = jnp.min(jnp.where(v == mx, blk_t, big), axis=0, keepdims=True)
        pick = blk_t == idx
        sel = jnp.where(pick, jnp.maximum(sel, jnp.where(mx > -jnp.inf, 1.0, 0.0)), sel)
        v = jnp.where(pick, -jnp.inf, v)
    return sel


def _lane_groups(s):
    return [s[:, j * LANES:(j + 1) * LANES] for j in range(s.shape[1] // LANES)]


def _one_ahead(items, make):
    nxt = make(items[0])
    for n, item in enumerate(items):
        cur = nxt
        if n + 1 < len(items):
            nxt = make(items[n + 1])
        yield item, cur


def _flash_rows(m_ref, l_ref, acc_ref, rows, s, v, mask_fn=None, sub=64, v_transposed=False):
    r = s.shape[0]
    ps, alphas = [], []
    for u in range(r // sub):
        rr = slice(rows.start + u * sub, rows.start + (u + 1) * sub)
        su = s[u * sub:(u + 1) * sub]
        if mask_fn is not None:
            su = jnp.where(mask_fn(rr), su, NEG)
        sj = _lane_groups(su)
        m_prev = m_ref[rr, :]
        m_new = jnp.maximum(m_prev, jnp.max(functools.reduce(jnp.maximum, sj), axis=-1, keepdims=True))
        alpha = jnp.exp2(m_prev - m_new)
        pj = [jnp.exp2(x - m_new) for x in sj]
        l_ref[rr, :] = alpha * l_ref[rr, :] + jnp.sum(functools.reduce(jnp.add, pj), axis=-1, keepdims=True)
        m_ref[rr, :] = m_new
        ps.append(jnp.concatenate([x.astype(BF16) for x in pj], axis=1))
        alphas.append(alpha)
    p = jnp.concatenate(ps, axis=0)
    pv = _nt(p, v) if v_transposed else _dot(p, v)
    acc_ref[rows, :] = jnp.concatenate(alphas, axis=0) * acc_ref[rows, :] + pv


def _softmax_weights(s, mask_fn, row0, sub=64):
    out = []
    for u in range(s.shape[0] // sub):
        su = jnp.where(mask_fn(slice(row0 + u * sub, row0 + (u + 1) * sub)), s[u * sub:(u + 1) * sub], -jnp.inf)
        sj = _lane_groups(su)
        m = jnp.max(functools.reduce(jnp.maximum, sj), axis=-1, keepdims=True)
        m = jnp.where(jnp.isfinite(m), m, 0.0)
        ej = [jnp.exp2(x - m) for x in sj]
        inv = 1.0 / jnp.maximum(jnp.sum(functools.reduce(jnp.add, ej), axis=-1, keepdims=True), 1e-30)
        out.append(jnp.concatenate([e * inv for e in ej], axis=1))
    return jnp.concatenate(out, axis=0)


def _softmax_pv_rows(s, v, mask_fn, row0, sub=64):
    r = s.shape[0]
    ps, ls = [], []
    for u in range(r // sub):
        su = jnp.where(mask_fn(slice(row0 + u * sub, row0 + (u + 1) * sub)), s[u * sub:(u + 1) * sub], NEG)
        sj = _lane_groups(su)
        m = jnp.max(functools.reduce(jnp.maximum, sj), axis=-1, keepdims=True)
        pj = [jnp.exp2(x - m) for x in sj]
        ls.append(jnp.broadcast_to(jnp.sum(functools.reduce(jnp.add, pj), axis=-1, keepdims=True), (sub, LANES)))
        ps.append(jnp.concatenate([x.astype(BF16) for x in pj], axis=1))
    return _dot(jnp.concatenate(ps, axis=0), v) / jnp.maximum(jnp.concatenate(ls, axis=0), 1e-30)


SEL_TK = 512
SEL_SUB = 64
SEL_UNROLL = 4
ROW_BLOCK = 512
WIN_ROW_BLOCK = 512
WIN_KEYS = WINDOW + Q_BLOCK


def _attn_prompt_kernel(q_ref, qn_ref, gn_ref, selk_ref, selv_ref, wink_ref, winv_ref, kcvc_ref, csel_ref, pq_ref,
                        o_ref, q2_ref, m_ref, l_ref, acc_ref, oc_ref, ow_ref, ps_ref, s0_ref, *, nq):
    i = pl.program_id(1)
    nrow = N_Q_HEADS * Q_BLOCK
    slot = i % 2
    blocks = [slice(r * ROW_BLOCK, (r + 1) * ROW_BLOCK) for r in range(nrow // ROW_BLOCK)]
    heads = [slice(c * Q_BLOCK, (c + 1) * Q_BLOCK) for c in range(N_Q_HEADS)]

    def rel(rr, n, step=1):
        shape = (rr.stop - rr.start, n)
        return lax.broadcasted_iota(jnp.int32, shape, 1) * step - lax.broadcasted_iota(jnp.int32, shape, 0)

    def qpos0(iq, rr):
        return iq * Q_BLOCK + rr.start % Q_BLOCK

    def prepare_cmp(qt_ref, iq, sl):
        qall = _dot(qt_ref[...], pq_ref[...]).astype(BF16)
        for c in range(N_Q_HEADS):
            q2_ref[sl, c * Q_BLOCK:(c + 1) * Q_BLOCK, 0:LANES] = qall[:, c * LANES:(c + 1) * LANES]
        kc = kcvc_ref[0, :, 0:LANES]
        vc = kcvc_ref[0, :, LANES:2 * LANES]
        n_c = kc.shape[0]
        for rows, s in _one_ahead(heads, lambda rr: _nt(q2_ref[sl, rr, 0:LANES], kc)):
            c = rows.start // Q_BLOCK
            p_c = _softmax_weights(
                s, lambda rr: rel(rr, n_c, CMP_STRIDE) <= qpos0(iq, rr) - (CMP_BLOCK - 1), rows.start)
            oc_ref[sl, rows, :] = _dot(p_c.astype(BF16), vc)
            prow = slice((c // GROUP) * Q_BLOCK, (c // GROUP + 1) * Q_BLOCK)
            if c % GROUP == 0:
                ps_ref[prow, :] = p_c
            else:
                ps_ref[prow, :] += p_c

    def prepare_sel(iq, sl):
        imp_t = _split_dot(ps_ref[...], csel_ref[...]).T
        blk_t = lax.broadcasted_iota(jnp.int32, imp_t.shape, 0)
        qpos_t = iq * Q_BLOCK + lax.broadcasted_iota(jnp.int32, imp_t.shape, 1) % Q_BLOCK
        forced_t = jnp.logical_or(blk_t == 0, blk_t == qpos_t // SEL_BLOCK)
        sel_t = _topk_mask_t(imp_t, forced_t, blk_t * SEL_BLOCK <= qpos_t, blk_t)
        bias = jnp.where(sel_t > 0.0, 0.0, NEG).T.astype(BF16)
        for c in range(N_Q_HEADS):
            h = c // GROUP
            q2_ref[sl, c * Q_BLOCK:(c + 1) * Q_BLOCK, LANES:2 * LANES] = bias[h * Q_BLOCK:(h + 1) * Q_BLOCK]

    @pl.when(i == 0)
    def _():
        prepare_cmp(q_ref, 0, 0)
        prepare_sel(0, 0)

    m_ref[...] = jnp.full(m_ref.shape, -jnp.inf, F32)
    l_ref[...] = jnp.zeros(l_ref.shape, F32)
    acc_ref[...] = jnp.zeros(acc_ref.shape, F32)

    def scores(rr, t):
        return _nt(q2_ref[slot, rr, :], selk_ref[pl.ds(pl.multiple_of(t * SEL_TK, SEL_TK), SEL_TK), :])

    def sel_step(t, diagonal):
        k0 = pl.multiple_of(t * SEL_TK, SEL_TK)
        v2 = selv_ref[pl.ds(k0, SEL_TK), :]
        mask_fn = (lambda rr: rel(rr, SEL_TK) <= qpos0(i, rr) - k0) if diagonal else None
        pending = s0_ref[...]
        for n, rows in enumerate(blocks):
            s = pending
            if n + 1 < len(blocks):
                pending = scores(blocks[n + 1], t)
            elif not diagonal:
                s0_ref[...] = scores(blocks[0], t + 1)
            _flash_rows(m_ref, l_ref, acc_ref, rows, s, v2, mask_fn, sub=SEL_SUB)

    n_below = (i * Q_BLOCK) // SEL_TK
    s0_ref[...] = scores(blocks[0], 0)

    def body(tt, carry):
        for k in range(SEL_UNROLL):
            sel_step(SEL_UNROLL * tt + k, False)
        return carry

    def tail(t, carry):
        sel_step(t, False)
        return carry

    n_trips = n_below // SEL_UNROLL
    lax.fori_loop(0, n_trips, body, 0)
    lax.fori_loop(n_trips * SEL_UNROLL, n_below, tail, 0)

    i_next = jnp.minimum(i + 1, nq - 1)
    sel_step(n_below, True)
    prepare_cmp(qn_ref, i_next, 1 - slot)
    prepare_sel(i_next, 1 - slot)

    w0 = pl.multiple_of(jnp.maximum(i - WINDOW // Q_BLOCK, 0) * Q_BLOCK, Q_BLOCK)
    kw = wink_ref[pl.ds(w0, WIN_KEYS), :]
    vw = winv_ref[pl.ds(w0, WIN_KEYS), :]

    def win_mask(rr):
        dist = (qpos0(i, rr) - w0) - rel(rr, WIN_KEYS)
        return pltpu.bitcast(dist, jnp.uint32) < WINDOW

    wblocks = [slice(r * WIN_ROW_BLOCK, (r + 1) * WIN_ROW_BLOCK) for r in range(nrow // WIN_ROW_BLOCK)]
    for rows, s in _one_ahead(wblocks, lambda rr: _nt(q2_ref[slot, rr, 0:LANES], kw)):
        ow_ref[rows, :] = _softmax_pv_rows(s, vw, win_mask, rows.start)

    gn = gn_ref[...]
    for c in range(N_Q_HEADS):
        h = c // GROUP
        rows = heads[c]
        o_s = acc_ref[rows, :] / jnp.maximum(l_ref[rows, :], 1e-30)
        o = (gn[:, 3 * c:3 * c + 1] * oc_ref[slot, rows, :] + gn[:, 3 * c + 1:3 * c + 2] * o_s
             + gn[:, 3 * c + 2:3 * c + 3] * ow_ref[rows, :])
        o_ref[:, c * HEAD_DIM:(c + 1) * HEAD_DIM] = o[:, h * HEAD_DIM:(h + 1) * HEAD_DIM]


def _q_placement():
    p = np.zeros((D_ATTN, N_Q_HEADS * LANES), np.float32)
    for c in range(N_Q_HEADS):
        h = c // GROUP
        for d in range(HEAD_DIM):
            p[c * HEAD_DIM + d, c * LANES + h * HEAD_DIM + d] = 1.0
    return jnp.asarray(p, BF16)


def _attn_prompt(q, gn, selk, selv, wink, winv, kcvc, batch, seq_len):
    nq = seq_len // Q_BLOCK
    n_c = kcvc.shape[1]
    assert seq_len % SEL_TK == 0 and seq_len // SEL_BLOCK <= LANES and seq_len >= WIN_KEYS and n_c % LANES == 0
    csel = jnp.asarray(_cmp_to_sel(n_c, n_c - 1, LANES), BF16)
    nrow = N_Q_HEADS * Q_BLOCK
    tile = lambda n: pl.BlockSpec((Q_BLOCK, n), lambda b, i: (b * nq + i, 0))
    next_tile = pl.BlockSpec((Q_BLOCK, D_ATTN), lambda b, i: (b * nq + jnp.minimum(i + 1, nq - 1), 0))
    seq = lambda n: pl.BlockSpec((seq_len, n), lambda b, i: (b, 0))
    rows_f32 = pltpu.VMEM((nrow, LANES), F32)
    return pl.pallas_call(
        functools.partial(_attn_prompt_kernel, nq=nq),
        grid=(batch, nq),
        in_specs=[tile(D_ATTN), next_tile, tile(LANES), seq(2 * LANES), seq(LANES), seq(LANES), seq(LANES),
                  pl.BlockSpec((1, n_c, 256), lambda b, i: (b, 0, 0)),
                  _full(csel.shape), _full((D_ATTN, N_Q_HEADS * LANES))],
        out_specs=tile(D_ATTN),
        out_shape=jax.ShapeDtypeStruct((batch * seq_len, D_ATTN), F32),
        scratch_shapes=[pltpu.VMEM((2, nrow, 2 * LANES), BF16), rows_f32, rows_f32, rows_f32,
                        pltpu.VMEM((2, nrow, LANES), F32), rows_f32,
                        pltpu.VMEM((N_KV_HEADS * Q_BLOCK, n_c), F32), pltpu.VMEM((ROW_BLOCK, SEL_TK), F32)],
        compiler_params=_cparams(("parallel", "arbitrary")),
        name="attn_prompt",
    )(q, q, gn, selk, selv, wink, winv, kcvc, csel, _q_placement())


SAMPLE_PAGES_PER_STEP = 32
SAMPLE_TK = 512


def _attn_sample_kernel(pt_ref, q_ref, gn_ref, kcvc_ref, csel_ref, pq_ref, oh_ref, selk_new_ref, selv_new_ref,
                        wink_ref, winv_ref, pages_ref, o_ref,
                        buf_ref, sem_ref, q1_ref, bias_ref, oc_ref, m_ref, l_ref, acc_ref, ow_ref,
                        *, n_groups, gp, past_len, ls):
    b = pl.program_id(0)
    j = pl.program_id(1)
    nrow = N_Q_HEADS * ls
    tk = gp * 128
    blocks_per_step = tk // SEL_BLOCK

    def copies(jj, slot):
        return [pltpu.make_async_copy(pages_ref.at[pt_ref[b, jj * gp + p], pl.ds(2, 2)],
                                      buf_ref.at[slot, p], sem_ref.at[slot])
                for p in range(gp)]

    @pl.when(j == 0)
    def _():
        for c in copies(0, 0):
            c.start()

    slot = j % 2

    @pl.when(j + 1 < n_groups)
    def _():
        for c in copies(j + 1, 1 - slot):
            c.start()

    tok = lax.broadcasted_iota(jnp.int32, (nrow, 1), 0) % ls
    qpos = past_len + tok

    @pl.when(j == 0)
    def _():
        qall = _dot(q_ref[0], pq_ref[...]).astype(BF16)
        q1 = jnp.concatenate([qall[:, c * LANES:(c + 1) * LANES] for c in range(N_Q_HEADS)], axis=0)
        q1_ref[...] = q1
        kc = kcvc_ref[0, :, 0:LANES]
        vc = kcvc_ref[0, :, LANES:2 * LANES]
        n_c = kc.shape[0]
        cmp_end = lax.broadcasted_iota(jnp.int32, (1, n_c), 1) * CMP_STRIDE + (CMP_BLOCK - 1)
        p_c = _softmax2_rows(_nt(q1, kc), cmp_end <= qpos)
        oc_ref[...] = _dot(p_c.astype(BF16), vc)
        hrows = GROUP * ls
        psum = jnp.concatenate(
            [sum(p_c[h * hrows + g * ls:h * hrows + (g + 1) * ls] for g in range(GROUP))
             for h in range(N_KV_HEADS)], axis=0)
        imp_t = _split_dot(psum, csel_ref[...]).T
        blk_t = lax.broadcasted_iota(jnp.int32, imp_t.shape, 0)
        qpos_t = past_len + lax.broadcasted_iota(jnp.int32, imp_t.shape, 1) % ls
        forced_t = jnp.logical_or(blk_t == 0, blk_t == qpos_t // SEL_BLOCK)
        sel_t = _topk_mask_t(imp_t, forced_t, blk_t * SEL_BLOCK <= qpos_t, blk_t)
        bias = jnp.where(sel_t > 0.0, 0.0, NEG).T.astype(BF16)
        for h in range(N_KV_HEADS):
            for g in range(GROUP):
                r0 = (h * GROUP + g) * ls
                bias_ref[r0:r0 + ls, :] = bias[h * ls:(h + 1) * ls]
        kpos_new = past_len + lax.broadcasted_iota(jnp.int32, (1, ls), 1)
        s = jnp.where(kpos_new <= qpos, _nt(q1, selk_new_ref[0]), NEG)
        m0 = jnp.max(s, axis=-1, keepdims=True)
        p = jnp.exp2(s - m0)
        m_ref[...] = jnp.broadcast_to(m0, m_ref.shape)
        l_ref[...] = jnp.broadcast_to(jnp.sum(p, axis=-1, keepdims=True), l_ref.shape)
        acc_ref[...] = _dot(p.astype(BF16), selv_new_ref[0])
        n_w = wink_ref.shape[1]
        kpos_w = past_len + ls - n_w + lax.broadcasted_iota(jnp.int32, (1, n_w), 1)
        dist = qpos - kpos_w
        mask_w = jnp.logical_and(jnp.logical_and(kpos_w >= 0, dist >= 0), dist < WINDOW)
        p_w = _softmax2_rows(_nt(q1, wink_ref[0]), mask_w)
        ow_ref[...] = _dot(p_w.astype(BF16), winv_ref[0])

    for c in copies(j, slot):
        c.wait()

    b0 = pl.multiple_of((j * blocks_per_step // LANES) * LANES, LANES)
    q2 = jnp.concatenate([bias_ref[:, pl.ds(b0, LANES)], q1_ref[...]], axis=1)
    ppt = SAMPLE_TK // 128

    def tile_t(t, kv):
        return jnp.concatenate([buf_ref[slot, t * ppt + p, kv].reshape(LANES, 128) for p in range(ppt)],
                               axis=1).astype(BF16)

    def scores(t):
        return _dot(q2, jnp.concatenate([oh_ref[:, t * SAMPLE_TK:(t + 1) * SAMPLE_TK], tile_t(t, 0)], axis=0))

    for t, s in _one_ahead(list(range(gp // ppt)), scores):
        _flash_rows(m_ref, l_ref, acc_ref, slice(0, nrow), s, tile_t(t, 1), sub=nrow, v_transposed=True)

    @pl.when(j == n_groups - 1)
    def _():
        gn = gn_ref[0]
        for c in range(N_Q_HEADS):
            h = c // GROUP
            rows = slice(c * ls, (c + 1) * ls)
            o_s = acc_ref[rows, :] / jnp.maximum(l_ref[rows, :], 1e-30)
            o = (gn[:, 3 * c:3 * c + 1] * oc_ref[rows, :] + gn[:, 3 * c + 1:3 * c + 2] * o_s
                 + gn[:, 3 * c + 2:3 * c + 3] * ow_ref[rows, :])
            o_ref[0, :, c * HEAD_DIM:(c + 1) * HEAD_DIM] = o[:, h * HEAD_DIM:(h + 1) * HEAD_DIM]


def _attn_sample(page_table, q, gn, kcvc, selk_new, selv_new, wink, winv, pages, past_len, ls):
    bs, n_pages = page_table.shape
    gp = min(SAMPLE_PAGES_PER_STEP, n_pages)
    n_groups = n_pages // gp
    tk = gp * 128
    blocks_per_step = tk // SEL_BLOCK
    assert LANES % blocks_per_step == 0 and tk % SAMPLE_TK == 0
    n_c = kcvc.shape[1]
    n_sel = -(-(past_len + ls) // SEL_BLOCK)
    n_sel_pad = -(-n_sel // LANES) * LANES
    csel = jnp.asarray(_cmp_to_sel(n_c, n_c - 1, n_sel_pad), BF16)
    key_blk = (np.arange(tk) // SEL_BLOCK)[None, :]
    onehot = [jnp.asarray((key_blk + s * blocks_per_step) % LANES == np.arange(LANES)[:, None], BF16)
              for s in range(LANES // blocks_per_step)]
    onehot = jnp.stack(onehot)
    n_rep = LANES // blocks_per_step
    nrow = N_Q_HEADS * ls
    kern = functools.partial(_attn_sample_kernel, n_groups=n_groups, gp=gp, past_len=past_len, ls=ls)
    per_seq = lambda *shape: pl.BlockSpec((1,) + shape, lambda b, j, pt: (b,) + (0,) * len(shape))
    const = lambda shape: pl.BlockSpec(shape, lambda b, j, pt: (0,) * len(shape))
    return pl.pallas_call(
        kern,
        grid_spec=pltpu.PrefetchScalarGridSpec(
            num_scalar_prefetch=1,
            grid=(bs, n_groups),
            in_specs=[per_seq(ls, D_ATTN), per_seq(ls, LANES), per_seq(n_c, 256),
                      const(csel.shape), const((D_ATTN, N_Q_HEADS * LANES)),
                      pl.BlockSpec((None, LANES, tk), lambda b, j, pt: (j % n_rep, 0, 0)),
                      per_seq(ls, LANES), per_seq(ls, LANES),
                      per_seq(wink.shape[1], LANES), per_seq(winv.shape[1], LANES),
                      pl.BlockSpec(memory_space=pl.ANY)],
            out_specs=per_seq(ls, D_ATTN),
            scratch_shapes=[pltpu.VMEM((2, gp, 2, N_KV_HEADS, HEAD_DIM, 128), F32),
                            pltpu.SemaphoreType.DMA((2,)),
                            pltpu.VMEM((nrow, LANES), BF16),
                            pltpu.VMEM((nrow, n_sel_pad), BF16),
                            pltpu.VMEM((nrow, LANES), F32),
                            pltpu.VMEM((nrow, LANES), F32),
                            pltpu.VMEM((nrow, LANES), F32),
                            pltpu.VMEM((nrow, LANES), F32),
                            pltpu.VMEM((nrow, LANES), F32)]),
        out_shape=jax.ShapeDtypeStruct((bs, ls, D_ATTN), F32),
        compiler_params=_cparams(("arbitrary", "arbitrary")),
        name="attn_sample",
    )(page_table, q, gn, kcvc, csel, _q_placement(), onehot, selk_new, selv_new, wink, winv, pages)


def _cmul(ar, ai, br, bi):
    return ar * br - ai * bi, ar * bi + ai * br


def _ssm_weights(lam_re, lam_im, log_dt, b_re, b_im, c_re, c_im, tc):
    dt = jnp.exp(log_dt)[:, None]
    mag = jnp.exp(lam_re * dt)
    a_re = mag * jnp.cos(lam_im * dt)
    a_im = mag * jnp.sin(lam_im * dt)
    den = lam_re * lam_re + lam_im * lam_im
    f_re = ((a_re - 1.0) * lam_re + a_im * lam_im) / den
    f_im = (a_im * lam_re - (a_re - 1.0) * lam_im) / den
    bb_re = f_re[..., None] * b_re - f_im[..., None] * b_im
    bb_im = f_re[..., None] * b_im + f_im[..., None] * b_re
    pr, pi = jnp.ones_like(a_re)[None], jnp.zeros_like(a_re)[None]
    sr, si = a_re, a_im
    while pr.shape[0] < tc + 1:
        nr, ni = _cmul(pr, pi, sr[None], si[None])
        pr, pi = jnp.concatenate([pr, nr]), jnp.concatenate([pi, ni])
        sr, si = _cmul(sr, si, sr, si)
    pr, pi = pr[:tc + 1], pi[:tc + 1]
    cpr = c_re[None] * pr[:, :, None, :] - c_im[None] * pi[:, :, None, :]
    cpi = c_re[None] * pi[:, :, None, :] + c_im[None] * pr[:, :, None, :]
    kker = jnp.einsum('tgop,gpi->gtoi', cpr[:tc], bb_re) - jnp.einsum('tgop,gpi->gtoi', cpi[:tc], bb_im)
    g = a_re.shape[0]
    nset = g // SSM_SET
    half = SSM_SET * SSM_STATE

    eye = jnp.eye(SSM_SET, dtype=F32)
    lag = np.arange(tc)[None, :] - np.arange(tc)[:, None]
    kpad = jnp.concatenate([kker, jnp.zeros_like(kker[:, :1])], axis=1)
    toep = kpad[:, np.where(lag >= 0, lag, tc)]
    t6 = jnp.transpose(toep, (0, 1, 4, 2, 3)).reshape(nset, SSM_SET, tc, SSM_GROUP, tc, SSM_GROUP)
    toep_s = jnp.einsum('jgsitc,gh->jsgithc', t6, eye).reshape(nset, tc * LANES, tc * LANES)
    qr, qi = pr[:tc][::-1], pi[:tc][::-1]
    n_re = qr[..., None] * bb_re[None] - qi[..., None] * bb_im[None]
    n_im = qr[..., None] * bb_im[None] + qi[..., None] * bb_re[None]
    n6 = jnp.transpose(jnp.stack([n_re, n_im], axis=2), (1, 0, 4, 2, 3))
    n6 = n6.reshape(nset, SSM_SET, tc, SSM_GROUP, 2, SSM_STATE)
    n_s = jnp.einsum('jgsirp,gh->jsgirhp', n6, eye).reshape(nset, tc * LANES, 2 * half)
    m6 = jnp.transpose(jnp.stack([cpr[1:tc + 1], -cpi[1:tc + 1]], axis=3), (1, 3, 4, 0, 2))
    m6 = m6.reshape(nset, SSM_SET, 2, SSM_STATE, tc, SSM_GROUP)
    m_s = jnp.einsum('jgrptc,gh->jrgpthc', m6, eye).reshape(nset, 2 * half, tc * LANES)
    return (toep_s.astype(BF16), n_s.astype(BF16), m_s.astype(BF16),
            pr[tc].reshape(1, nset * half), pi[tc].reshape(1, nset * half))


def _chunk_rows(u_ref, tc):
    n = u_ref.shape[0] // tc
    return jnp.concatenate([u_ref[pl.ds(s, n, stride=tc), :] for s in range(tc)], axis=1)


def _ssm_state_kernel(u_ref, n_ref, s_ref, *, tc):
    s_ref[...] = _dot(_chunk_rows(u_ref, tc).astype(BF16), n_ref[0])


def _ssm_carry_kernel(s_ref, h0_ref, are_ref, aim_ref, hs_ref, f_ref, h_scr, *, batch, nset):
    half = SSM_SET * SSM_STATE

    @pl.when(pl.program_id(0) == 0)
    def _():
        h_scr[...] = h0_ref[...]

    def body(k, h):
        sk = jnp.concatenate([s_ref[b, pl.ds(k, 1), :] for b in range(batch)], axis=0)
        for b in range(batch):
            hs_ref[b, pl.ds(k, 1), :] = h[b:b + 1]
        out = []
        for j in range(nset):
            c0 = 2 * j * half
            re, im = h[:, c0:c0 + half], h[:, c0 + half:c0 + 2 * half]
            ar, ai = are_ref[:, j * half:(j + 1) * half], aim_ref[:, j * half:(j + 1) * half]
            out.append(ar * re - ai * im + sk[:, c0:c0 + half])
            out.append(ar * im + ai * re + sk[:, c0 + half:c0 + 2 * half])
        return jnp.concatenate(out, axis=1)

    h = lax.fori_loop(0, s_ref.shape[1], body, h_scr[...])
    h_scr[...] = h
    f_ref[...] = h


def _ssm_out_kernel(u_ref, t_ref, m_ref, h_ref, y_ref, *, tc):
    y = _dot(_chunk_rows(u_ref, tc).astype(BF16), t_ref[0]) + _dot(h_ref[...].astype(BF16), m_ref[0])
    n = y.shape[0]
    for s in range(tc):
        y_ref[pl.ds(s, n, stride=tc), :] = y[:, s * LANES:(s + 1) * LANES]


def _ssm(u2d, batch, seq_len, tc, h0_re, h0_im, ssm_w):
    toep, nmat, mmat, at_re, at_im = ssm_w
    nset = toep.shape[0]
    half = SSM_SET * SSM_STATE
    ncol = 2 * nset * half
    t_tok = batch * seq_len
    nck = seq_len // tc
    tm = min(2048, t_tok)
    rt = tm // tc
    grid = (nset, t_tok // tm)
    u_spec = pl.BlockSpec((tm, LANES), lambda j, i: (i, j))
    w_spec = lambda w: pl.BlockSpec((1,) + w.shape[1:], lambda j, i: (j, 0, 0))
    st_spec = pl.BlockSpec((rt, 2 * half), lambda j, i: (i, j))
    s = pl.pallas_call(
        functools.partial(_ssm_state_kernel, tc=tc),
        grid=grid,
        in_specs=[u_spec, w_spec(nmat)],
        out_specs=st_spec,
        out_shape=jax.ShapeDtypeStruct((batch * nck, ncol), F32),
        compiler_params=_cparams(("parallel", "parallel")),
        name="ssm_state",
    )(u2d, nmat)
    kb = min(128, nck)
    h0 = jnp.stack([h0_re.reshape(batch, nset, half), h0_im.reshape(batch, nset, half)], axis=2).reshape(batch, ncol)
    seq3 = pl.BlockSpec((batch, kb, ncol), lambda c: (0, c, 0))
    hs, f = pl.pallas_call(
        functools.partial(_ssm_carry_kernel, batch=batch, nset=nset),
        grid=(nck // kb,),
        in_specs=[seq3, _full((batch, ncol)), _full(at_re.shape), _full(at_im.shape)],
        out_specs=[seq3, _full((batch, ncol))],
        out_shape=[jax.ShapeDtypeStruct((batch, nck, ncol), F32), jax.ShapeDtypeStruct((batch, ncol), F32)],
        scratch_shapes=[pltpu.VMEM((batch, ncol), F32)],
        compiler_params=_cparams(("arbitrary",)),
        name="ssm_carry",
    )(s.reshape(batch, nck, ncol), h0, at_re, at_im)
    y = pl.pallas_call(
        functools.partial(_ssm_out_kernel, tc=tc),
        grid=grid,
        in_specs=[u_spec, w_spec(toep), w_spec(mmat), st_spec],
        out_specs=u_spec,
        out_shape=jax.ShapeDtypeStruct(u2d.shape, F32),
        compiler_params=_cparams(("parallel", "parallel")),
        name="ssm_out",
    )(u2d, toep, mmat, hs.reshape(batch * nck, ncol))
    f = f.reshape(batch, nset, 2, SSM_SET, SSM_STATE)
    g = nset * SSM_SET
    return y, f[:, :, 0].reshape(batch, g, SSM_STATE), f[:, :, 1].reshape(batch, g, SSM_STATE)


def _post_kernel(x_ref, oa_ref, ys_ref, u_ref, g_ref, fg_ref, d_ref, bglu_ref,
                 wza_ref, wzs_ref, wgm_ref, wglu_ref, wla_ref, wls_ref, wo_ref, out_ref):
    x = x_ref[...]
    d = x.shape[1]
    xn = x * lax.rsqrt(jnp.mean(x * x, axis=-1, keepdims=True) + RMS_EPS) * g_ref[...]
    xb = xn.astype(BF16)
    z_a = _dot(xb, wza_ref[...])
    z_s = _dot(xb, wzs_ref[...])
    gm = jax.nn.sigmoid(_dot(xb, wgm_ref[...]))
    branch_a = _dot((oa_ref[...] * jax.nn.silu(z_a)).astype(BF16), wla_ref[...])
    y = jax.nn.gelu(ys_ref[...] + d_ref[...] * u_ref[...])
    y = y * jax.nn.sigmoid(_dot(y.astype(BF16), wglu_ref[...]) + bglu_ref[...])
    branch_b = _dot((y * jax.nn.silu(z_s)).astype(BF16), wls_ref[...])
    merged = gm[:, :d] * branch_a + gm[:, d:] * branch_b
    r = x + _dot(merged.astype(BF16), wo_ref[...])
    out_ref[...] = r * lax.rsqrt(jnp.mean(r * r, axis=-1, keepdims=True) + RMS_EPS) * fg_ref[...]


def _post(x2d, o_attn, y_ssm, u, norm_g, final_g, ssm_d, b_glu, wza, wzs, wgm, wglu, wla, wls, wo):
    t, d = x2d.shape
    tm = min(256, t)
    row = lambda n: pl.BlockSpec((tm, n), lambda i: (i, 0))
    ws = [wza, wzs, wgm, wglu, wla, wls, wo]
    vecs = [norm_g.reshape(1, d), final_g.reshape(1, d), ssm_d.reshape(1, -1), b_glu.reshape(1, -1)]
    return pl.pallas_call(
        _post_kernel,
        grid=(t // tm,),
        in_specs=[row(d), row(o_attn.shape[1]), row(y_ssm.shape[1]), row(u.shape[1])]
                 + [_full(v.shape) for v in vecs] + [_full(w.shape) for w in ws],
        out_specs=row(d),
        out_shape=jax.ShapeDtypeStruct((t, d), F32),
        compiler_params=_cparams(("parallel",)),
        name="post",
    )(x2d, o_attn, y_ssm, u, *vecs, *ws)


SSM_CHUNK = 8


def kernel(x_prompt, x_sample, cache_kv, cache_win_kv, state_ssm_re, state_ssm_im, page_table, norm_g, w_in, cmp_pe, cmp_w1, cmp_b1, cmp_w2, ssm_lam_re, ssm_lam_im, ssm_log_dt, ssm_b_re, ssm_b_im, ssm_c_re, ssm_c_im, ssm_d, w_glu, b_glu, w_lift_attn, w_lift_ssm, w_out, final_g):
    depth = norm_g.shape[0]
    assert depth == 1
    l = 0
    bp, lp, d = x_prompt.shape
    bs, ls, _ = x_sample.shape
    n_pages, page = page_table.shape[1], cache_kv.shape[2]
    past_len = n_pages * page
    w_buf = cache_win_kv.shape[2]
    d_ssm = ssm_d.shape[1]
    n_grp = d_ssm // SSM_GROUP
    assert page == 128 and w_buf == WINDOW and lp % Q_BLOCK == 0 and lp >= WINDOW

    splits = (D_ATTN, N_KV_SLOTS * N_KV_HEADS * HEAD_DIM, 3 * N_Q_HEADS, D_ATTN, d_ssm, d_ssm, 2 * d)
    offs = np.concatenate([[0], np.cumsum(splits)])
    wb = w_in[l].astype(BF16)
    wq, wkv, wg, wza, wu, wzs, wgm = [wb[:, offs[k]:offs[k + 1]] for k in range(7)]
    wg = jnp.pad(wg, ((0, 0), (0, LANES - wg.shape[1])))
    cw1, cb, cw2 = _compress_weights(cmp_pe[l], cmp_w1[l], cmp_b1[l], cmp_w2[l])
    ssm_p = (ssm_lam_re[l], ssm_lam_im[l], ssm_log_dt[l], ssm_b_re[l], ssm_b_im[l], ssm_c_re[l], ssm_c_im[l])
    post_w = (norm_g[l], final_g, ssm_d[l], b_glu[l], wza, wzs, wgm, w_glu[l].astype(BF16),
              w_lift_attn[l].astype(BF16), w_lift_ssm[l].astype(BF16), w_out[l].astype(BF16))

    xp = x_prompt.reshape(bp * lp, d)
    q, kv_t, win_t, cmp_rows, selk, selv, wink, winv, gn, u = _pre(xp, norm_g[l], wq, wkv, wg, wu, lp, True)
    ident = jnp.arange(bp * (lp // 128), dtype=jnp.int32).reshape(bp, lp // 128)
    kcvc = _compress(ident, cmp_rows.reshape(bp * lp // 128, 128, 2 * LANES), cw1, cb, cw2)
    o_attn = _attn_prompt(q, gn, selk, selv, wink, winv, kcvc, bp, lp)
    assert lp % SSM_CHUNK == 0 and ls == SSM_CHUNK
    ssm_w = _ssm_weights(*ssm_p, SSM_CHUNK)
    h0 = jnp.zeros((bp, n_grp, SSM_STATE), F32)
    y_ssm, hr_p, hi_p = _ssm(u, bp, lp, SSM_CHUNK, h0, h0, ssm_w)
    y_prompt = _post(xp, o_attn, y_ssm, u, *post_w).reshape(bp, lp, d)
    kv_prompt = jnp.transpose(kv_t.reshape(bp, 4, N_KV_HEADS, HEAD_DIM, lp), (0, 4, 1, 2, 3))[None]
    win_t = win_t.reshape(bp, 2, N_KV_HEADS, HEAD_DIM, lp)[..., lp - WINDOW:]
    win_prompt = jnp.transpose(win_t, (0, 4, 1, 2, 3))[None]

    xs = x_sample.reshape(bs * ls, d)
    q, kv, win, _, selk, selv, wink, winv, gn, u = _pre(xs, norm_g[l], wq, wkv, wg, wu, ls, False)
    pages = jnp.transpose(cache_kv[l], (0, 2, 3, 4, 1))
    kcvc = _compress(page_table, pages, cw1, cb, cw2)
    cwin = cache_win_kv[l].reshape(bs, w_buf, 2 * LANES)
    wk_all = jnp.concatenate([cwin[:, :, :LANES].astype(BF16), wink.reshape(bs, ls, LANES)], axis=1)
    wv_all = jnp.concatenate([cwin[:, :, LANES:].astype(BF16), winv.reshape(bs, ls, LANES)], axis=1)
    o_attn = _attn_sample(page_table, q.reshape(bs, ls, D_ATTN), gn.reshape(bs, ls, LANES), kcvc,
                          selk[:, :LANES].reshape(bs, ls, LANES), selv.reshape(bs, ls, LANES),
                          wk_all, wv_all, pages, past_len, ls)
    y_ssm, hr_s, hi_s = _ssm(u, bs, ls, SSM_CHUNK, state_ssm_re[l], state_ssm_im[l], ssm_w)
    y_sample = _post(xs, o_attn.reshape(bs * ls, D_ATTN), y_ssm, u, *post_w).reshape(bs, ls, d)
    kv_sample = kv.reshape(1, bs, ls, 4, N_KV_HEADS, HEAD_DIM)
    win_new = win.reshape(bs, ls, 2, N_KV_HEADS, HEAD_DIM)
    win_sample = jnp.concatenate([cache_win_kv[l], win_new], axis=1)[None, :, ls:]

    return (y_prompt, y_sample, kv_prompt, win_prompt, hr_p[None], hi_p[None],
            kv_sample, win_sample, hr_s[None], hi_s[None])
```

```python
import functools
import math

import numpy as np
import jax
import jax.numpy as jnp
from jax import lax
from jax.experimental import pallas as pl
from jax.experimental.pallas import tpu as pltpu

F32 = jnp.float32
BF16 = jnp.bfloat16

HEAD_DIM = 64
N_KV_HEADS = 2
GROUP = 4
N_Q_HEADS = N_KV_HEADS * GROUP
D_ATTN = N_Q_HEADS * HEAD_DIM
N_KV_SLOTS = 6
CMP_BLOCK = 32
CMP_STRIDE = 16
SEL_BLOCK = 64
N_SEL = 16
WINDOW = 512
Q_BLOCK = 128
SSM_GROUP = 16
SSM_STATE = 64
SSM_SET = 8
RMS_EPS = 1e-6
LANES = 128
KV_ROW = 4 * N_KV_HEADS * HEAD_DIM
NEG = -1e30
LOG2E = math.log2(math.e)
VMEM_LIMIT = 56 * 1024 * 1024


def _cparams(sem):
    return pltpu.CompilerParams(dimension_semantics=sem, vmem_limit_bytes=VMEM_LIMIT)


def _full(shape):
    nd = len(shape)
    return pl.BlockSpec(shape, lambda *_: (0,) * nd)


def _nt(a, b):
    return lax.dot_general(a, b, (((1,), (1,)), ((), ())), preferred_element_type=F32)


def _dot(a, b):
    return jnp.dot(a, b, preferred_element_type=F32)


def _pre_kernel(x_ref, g_ref, wq_ref, wkv_ref, wg_ref, wu_ref,
                q_ref, kv_ref, win_ref, cmp_ref, selk_ref, selv_ref, wink_ref, winv_ref, gn_ref, u_ref,
                *, seq_len, tm, transposed_kv):
    x = x_ref[...]
    xn = x * lax.rsqrt(jnp.mean(x * x, axis=-1, keepdims=True) + RMS_EPS) * g_ref[...]
    xb = xn.astype(BF16)
    q = _dot(xb, wq_ref[...])
    q_ref[...] = (q * (LOG2E * HEAD_DIM ** -0.5)).astype(BF16)
    kv = _dot(xb, wkv_ref[...])
    if transposed_kv:
        for c in range(4):
            kv_ref[c * LANES:(c + 1) * LANES, :] = kv[:, c * LANES:(c + 1) * LANES].T
        for c in range(2):
            win_ref[c * LANES:(c + 1) * LANES, :] = kv[:, KV_ROW + c * LANES:KV_ROW + (c + 1) * LANES].T
    else:
        kv_ref[...] = kv[:, :KV_ROW]
        win_ref[...] = kv[:, KV_ROW:]
    cmp_ref[...] = kv[:, 0:2 * LANES]
    pos = (pl.program_id(0) * tm + lax.broadcasted_iota(jnp.int32, (tm, LANES), 0)) % seq_len
    onehot = lax.broadcasted_iota(jnp.int32, (tm, LANES), 1) == (pos // SEL_BLOCK) % LANES
    selk_ref[:, :LANES] = kv[:, 256:384].astype(BF16)
    selk_ref[:, LANES:] = jnp.where(onehot, 1.0, 0.0).astype(BF16)
    selv_ref[...] = kv[:, 384:512].astype(BF16)
    wink_ref[...] = kv[:, 512:640].astype(BF16)
    winv_ref[...] = kv[:, 640:768].astype(BF16)
    gn_ref[...] = jax.nn.sigmoid(_dot(xb, wg_ref[...]))
    u_ref[...] = _dot(xb, wu_ref[...])


def _pre(x2d, norm_g, wq, wkv, wg, wu, seq_len, transposed_kv):
    t, d = x2d.shape
    tm = min(256, t)
    kern = functools.partial(_pre_kernel, seq_len=seq_len, tm=tm, transposed_kv=transposed_kv)
    row = lambda n: pl.BlockSpec((tm, n), lambda i: (i, 0))
    outs = [(D_ATTN, BF16), (KV_ROW, F32), (2 * LANES, F32), (2 * LANES, F32), (2 * LANES, BF16), (LANES, BF16),
            (LANES, BF16), (LANES, BF16), (LANES, F32), (wu.shape[1], F32)]
    out_specs = [row(n) for n, _ in outs]
    out_shape = [jax.ShapeDtypeStruct((t, n), dt) for n, dt in outs]
    if transposed_kv:
        per_seq = seq_len // tm
        for k in (1, 2):
            n = outs[k][0]
            out_specs[k] = pl.BlockSpec((None, n, tm), lambda i: (i // per_seq, 0, i % per_seq))
            out_shape[k] = jax.ShapeDtypeStruct((t // seq_len, n, seq_len), F32)
    return pl.pallas_call(
        kern,
        grid=(t // tm,),
        in_specs=[row(d), _full((1, d)), _full(wq.shape), _full(wkv.shape), _full(wg.shape), _full(wu.shape)],
        out_specs=out_specs,
        out_shape=out_shape,
        compiler_params=_cparams(("parallel",)),
        name="pre",
    )(x2d, norm_g.reshape(1, d), wq, wkv, wg, wu)


PAGES_PER_STEP = 32
CMP_PAGE_PARTS = 2


def _compress_kernel(pt_ref, pages_ref, w1_ref, b_ref, w2_ref, out_ref, buf_ref, sem_ref, h_ref, *tbuf_ref,
                     n_groups, gp, transposed):
    b = pl.program_id(0)
    j = pl.program_id(1)
    rows = gp * (128 // CMP_STRIDE)

    def copies(jj, slot):
        if transposed:
            return [pltpu.make_async_copy(pages_ref.at[pt_ref[b, jj * gp + p], pl.ds(0, 2)],
                                          tbuf_ref[0].at[slot, p], sem_ref.at[slot])
                    for p in range(gp)]
        return [pltpu.make_async_copy(pages_ref.at[pt_ref[b, jj * gp + p], :, pl.ds(half * LANES, LANES)],
                                      buf_ref.at[slot, half, p], sem_ref.at[slot])
                for p in range(gp) for half in range(2)]

    @pl.when(j == 0)
    def _():
        for c in copies(0, 0):
            c.start()

    slot = j % 2

    @pl.when(j + 1 < n_groups)
    def _():
        for c in copies(j + 1, 1 - slot):
            c.start()

    for c in copies(j, slot):
        c.wait()

    n_part = CMP_PAGE_PARTS if transposed and gp % CMP_PAGE_PARTS == 0 else 1
    pp = gp // n_part
    prows = rows // n_part
    rslot = 0 if transposed else slot
    for part in range(n_part):
        p0 = part * pp
        if transposed:
            for p in range(p0, p0 + pp):
                for half in range(2):
                    buf_ref[0, half, p] = tbuf_ref[0][slot, p, half].reshape(LANES, 128).T
        acc = jnp.zeros((prows, 512), F32)
        for jj in range(CMP_STRIDE):
            xj = [buf_ref[rslot, half, p0:p0 + pp, pl.ds(jj, 128 // CMP_STRIDE, stride=CMP_STRIDE), :]
                  .reshape(prows, LANES) for half in range(2)]
            xj = jnp.concatenate(xj, axis=1).astype(BF16)
            acc = acc + _dot(xj, w1_ref[jj])
        h_ref[pl.ds(pl.multiple_of(j * rows + part * prows, prows), prows), :] = acc

    @pl.when(j == n_groups - 1)
    def _():
        n = h_ref.shape[0]
        hb = pltpu.roll(h_ref[:, 256:512], n - 1, 0)
        hid = h_ref[:, 0:256] + hb + b_ref[...]
        act = jax.nn.gelu(hid).astype(BF16)
        out_ref[0] = _dot(act, w2_ref[...]).astype(BF16)


def _compress(page_table, pages, w1, b_eff, w2):
    n_seq, n_pages = page_table.shape
    gp = min(PAGES_PER_STEP, n_pages)
    n_groups = n_pages // gp
    n_chunks = n_pages * (128 // CMP_STRIDE)
    transposed = pages.ndim == 5
    kern = functools.partial(_compress_kernel, n_groups=n_groups, gp=gp, transposed=transposed)
    scratch = [pltpu.VMEM((1 if transposed else 2, 2, gp, 128, LANES), F32),
               pltpu.SemaphoreType.DMA((2,)),
               pltpu.VMEM((n_chunks, 512), F32)]
    if transposed:
        scratch.append(pltpu.VMEM((2, gp, 2, N_KV_HEADS, HEAD_DIM, 128), F32))
    return pl.pallas_call(
        kern,
        grid_spec=pltpu.PrefetchScalarGridSpec(
            num_scalar_prefetch=1,
            grid=(n_seq, n_groups),
            in_specs=[pl.BlockSpec(memory_space=pl.ANY),
                      pl.BlockSpec(w1.shape, lambda b, j, pt: (0, 0, 0)),
                      pl.BlockSpec(b_eff.shape, lambda b, j, pt: (0, 0)),
                      pl.BlockSpec(w2.shape, lambda b, j, pt: (0, 0))],
            out_specs=pl.BlockSpec((1, n_chunks, 256), lambda b, j, pt: (b, 0, 0)),
            scratch_shapes=scratch),
        out_shape=jax.ShapeDtypeStruct((n_seq, n_chunks, 256), BF16),
        compiler_params=_cparams(("arbitrary", "arbitrary")),
        name="compress",
    )(page_table, pages, w1, b_eff, w2)


def _compress_weights(cmp_pe, cmp_w1, cmp_b1, cmp_w2):
    w1 = cmp_w1.reshape(2, 2, CMP_STRIDE, HEAD_DIM, HEAD_DIM)
    eye = jnp.eye(N_KV_HEADS, dtype=F32)
    blk = jnp.einsum('vrjde,vw,hg->rjvhdwge', w1, jnp.eye(2, dtype=F32), eye)
    blk = blk.reshape(2, CMP_STRIDE, 256, 256)
    w1_all = jnp.concatenate([blk[0], blk[1]], axis=-1).astype(BF16)
    pe = cmp_pe.reshape(2, 2, CMP_STRIDE, HEAD_DIM)
    pe_bias = jnp.einsum('vrjd,vrjde->ve', pe, w1) + cmp_b1
    b_eff = jnp.broadcast_to(pe_bias[:, None, :], (2, N_KV_HEADS, HEAD_DIM)).reshape(1, 256)
    w2 = jnp.einsum('ved,vw,hg->vhewgd', cmp_w2, jnp.eye(2, dtype=F32), eye).reshape(256, 256).astype(BF16)
    return w1_all, b_eff, w2


def _cmp_to_sel(n_cmp_rows, n_valid, n_sel_pad):
    c0 = np.arange(n_cmp_rows)[:, None] * CMP_STRIDE
    s0 = np.arange(n_sel_pad)[None, :] * SEL_BLOCK
    shared = np.minimum(c0 + CMP_BLOCK, s0 + SEL_BLOCK) - np.maximum(c0, s0)
    m = np.clip(shared, 0, None).astype(np.float32) / CMP_BLOCK
    m[n_valid:] = 0.0
    return m


def _softmax2_rows(s, mask):
    s = jnp.where(mask, s, -jnp.inf)
    m = jnp.max(s, axis=-1, keepdims=True)
    m = jnp.where(jnp.isfinite(m), m, 0.0)
    e = jnp.exp2(s - m)
    return e / jnp.maximum(jnp.sum(e, axis=-1, keepdims=True), 1e-30)


def _split_dot(p, w):
    hi = p.astype(BF16)
    lo = (p - hi.astype(F32)).astype(BF16)
    return _dot(hi, w) + _dot(lo, w)


def _topk_mask_t(imp_t, forced_t, causal_t, blk_t):
    v = jnp.where(forced_t, jnp.inf, jnp.where(causal_t, imp_t, -jnp.inf))
    big = jnp.int32(imp_t.shape[0])
    sel = jnp.zeros(imp_t.shape, F32)
    for _ in range(N_SEL):
        mx = jnp.max(v, axis=0, keepdims=True)
        idx = jnp.min(jnp.where(v == mx, blk_t, big), axis=0, keepdims=True)
        pick = blk_t == idx
        sel = jnp.where(pick, jnp.maximum(sel, jnp.where(mx > -jnp.inf, 1.0, 0.0)), sel)
        v = jnp.where(pick, -jnp.inf, v)
    return sel


def _lane_groups(s):
    return [s[:, j * LANES:(j + 1) * LANES] for j in range(s.shape[1] // LANES)]


def _one_ahead(items, make):
    nxt = make(items[0])
    for n, item in enumerate(items):
        cur = nxt
        if n + 1 < len(items):
            nxt = make(items[n + 1])
        yield item, cur


def _flash_rows(m_ref, l_ref, acc_ref, rows, s, v, mask_fn=None, sub=64, v_transposed=False):
    r = s.shape[0]
    ps, alphas = [], []
    for u in range(r // sub):
        rr = slice(rows.start + u * sub, rows.start + (u + 1) * sub)
        su = s[u * sub:(u + 1) * sub]
        if mask_fn is not None:
            su = jnp.where(mask_fn(rr), su, NEG)
        sj = _lane_groups(su)
        m_prev = m_ref[rr, :]
        m_new = jnp.maximum(m_prev, jnp.max(functools.reduce(jnp.maximum, sj), axis=-1, keepdims=True))
        alpha = jnp.exp2(m_prev - m_new)
        pj = [jnp.exp2(x - m_new) for x in sj]
        l_ref[rr, :] = alpha * l_ref[rr, :] + jnp.sum(functools.reduce(jnp.add, pj), axis=-1, keepdims=True)
        m_ref[rr, :] = m_new
        ps.append(jnp.concatenate([x.astype(BF16) for x in pj], axis=1))
        alphas.append(alpha)
    p = jnp.concatenate(ps, axis=0)
    pv = _nt(p, v) if v_transposed else _dot(p, v)
    acc_ref[rows, :] = jnp.concatenate(alphas, axis=0) * acc_ref[rows, :] + pv


def _softmax_weights(s, mask_fn, row0, sub=64):
    out = []
    for u in range(s.shape[0] // sub):
        su = jnp.where(mask_fn(slice(row0 + u * sub, row0 + (u + 1) * sub)), s[u * sub:(u + 1) * sub], -jnp.inf)
        sj = _lane_groups(su)
        m = jnp.max(functools.reduce(jnp.maximum, sj), axis=-1, keepdims=True)
        m = jnp.where(jnp.isfinite(m), m, 0.0)
        ej = [jnp.exp2(x - m) for x in sj]
        inv = 1.0 / jnp.maximum(jnp.sum(functools.reduce(jnp.add, ej), axis=-1, keepdims=True), 1e-30)
        out.append(jnp.concatenate([e * inv for e in ej], axis=1))
    return jnp.concatenate(out, axis=0)


def _softmax_pv_rows(s, v, mask_fn, row0, sub=64):
    r = s.shape[0]
    ps, ls = [], []
    for u in range(r // sub):
        su = jnp.where(mask_fn(slice(row0 + u * sub, row0 + (u + 1) * sub)), s[u * sub:(u + 1) * sub], NEG)
        sj = _lane_groups(su)
        m = jnp.max(functools.reduce(jnp.maximum, sj), axis=-1, keepdims=True)
        pj = [jnp.exp2(x - m) for x in sj]
        ls.append(jnp.broadcast_to(jnp.sum(functools.reduce(jnp.add, pj), axis=-1, keepdims=True), (sub, LANES)))
        ps.append(jnp.concatenate([x.astype(BF16) for x in pj], axis=1))
    return _dot(jnp.concatenate(ps, axis=0), v) / jnp.maximum(jnp.concatenate(ls, axis=0), 1e-30)


SEL_TK = 512
SEL_SUB = 64
SEL_UNROLL = 4
ROW_BLOCK = 512
WIN_ROW_BLOCK = 512
WIN_KEYS = WINDOW + Q_BLOCK


def _attn_prompt_kernel(q_ref, qn_ref, gn_ref, selk_ref, selv_ref, wink_ref, winv_ref, kcvc_ref, csel_ref, pq_ref,
                        o_ref, q2_ref, m_ref, l_ref, acc_ref, oc_ref, ow_ref, ps_ref, s0_ref, *, nq):
    i = pl.program_id(1)
    nrow = N_Q_HEADS * Q_BLOCK
    slot = i % 2
    blocks = [slice(r * ROW_BLOCK, (r + 1) * ROW_BLOCK) for r in range(nrow // ROW_BLOCK)]
    heads = [slice(c * Q_BLOCK, (c + 1) * Q_BLOCK) for c in range(N_Q_HEADS)]

    def rel(rr, n, step=1):
        shape = (rr.stop - rr.start, n)
        return lax.broadcasted_iota(jnp.int32, shape, 1) * step - lax.broadcasted_iota(jnp.int32, shape, 0)

    def qpos0(iq, rr):
        return iq * Q_BLOCK + rr.start % Q_BLOCK

    def prepare_cmp(qt_ref, iq, sl):
        qall = _dot(qt_ref[...], pq_ref[...]).astype(BF16)
        for c in range(N_Q_HEADS):
            q2_ref[sl, c * Q_BLOCK:(c + 1) * Q_BLOCK, 0:LANES] = qall[:, c * LANES:(c + 1) * LANES]
        kc = kcvc_ref[0, :, 0:LANES]
        vc = kcvc_ref[0, :, LANES:2 * LANES]
        n_c = kc.shape[0]
        for rows, s in _one_ahead(heads, lambda rr: _nt(q2_ref[sl, rr, 0:LANES], kc)):
            c = rows.start // Q_BLOCK
            p_c = _softmax_weights(
                s, lambda rr: rel(rr, n_c, CMP_STRIDE) <= qpos0(iq, rr) - (CMP_BLOCK - 1), rows.start)
            oc_ref[sl, rows, :] = _dot(p_c.astype(BF16), vc)
            prow = slice((c // GROUP) * Q_BLOCK, (c // GROUP + 1) * Q_BLOCK)
            if c % GROUP == 0:
                ps_ref[prow, :] = p_c
            else:
                ps_ref[prow, :] += p_c

    def prepare_sel(iq, sl):
        imp_t = _split_dot(ps_ref[...], csel_ref[...]).T
        blk_t = lax.broadcasted_iota(jnp.int32, imp_t.shape, 0)
        qpos_t = iq * Q_BLOCK + lax.broadcasted_iota(jnp.int32, imp_t.shape, 1) % Q_BLOCK
        forced_t = jnp.logical_or(blk_t == 0, blk_t == qpos_t // SEL_BLOCK)
        sel_t = _topk_mask_t(imp_t, forced_t, blk_t * SEL_BLOCK <= qpos_t, blk_t)
        bias = jnp.where(sel_t > 0.0, 0.0, NEG).T.astype(BF16)
        for c in range(N_Q_HEADS):
            h = c // GROUP
            q2_ref[sl, c * Q_BLOCK:(c + 1) * Q_BLOCK, LANES:2 * LANES] = bias[h * Q_BLOCK:(h + 1) * Q_BLOCK]

    @pl.when(i == 0)
    def _():
        prepare_cmp(q_ref, 0, 0)
        prepare_sel(0, 0)

    m_ref[...] = jnp.full(m_ref.shape, -jnp.inf, F32)
    l_ref[...] = jnp.zeros(l_ref.shape, F32)
    acc_ref[...] = jnp.zeros(acc_ref.shape, F32)

    def scores(rr, t):
        return _nt(q2_ref[slot, rr, :], selk_ref[pl.ds(pl.multiple_of(t * SEL_TK, SEL_TK), SEL_TK), :])

    def sel_step(t, diagonal):
        k0 = pl.multiple_of(t * SEL_TK, SEL_TK)
        v2 = selv_ref[pl.ds(k0, SEL_TK), :]
        mask_fn = (lambda rr: rel(rr, SEL_TK) <= qpos0(i, rr) - k0) if diagonal else None
        pending = s0_ref[...]
        for n, rows in enumerate(blocks):
            s = pending
            if n + 1 < len(blocks):
                pending = scores(blocks[n + 1], t)
            elif not diagonal:
                s0_ref[...] = scores(blocks[0], t + 1)
            _flash_rows(m_ref, l_ref, acc_ref, rows, s, v2, mask_fn, sub=SEL_SUB)

    n_below = (i * Q_BLOCK) // SEL_TK
    s0_ref[...] = scores(blocks[0], 0)

    def body(tt, carry):
        for k in range(SEL_UNROLL):
            sel_step(SEL_UNROLL * tt + k, False)
        return carry

    def tail(t, carry):
        sel_step(t, False)
        return carry

    n_trips = n_below // SEL_UNROLL
    lax.fori_loop(0, n_trips, body, 0)
    lax.fori_loop(n_trips * SEL_UNROLL, n_below, tail, 0)

    i_next = jnp.minimum(i + 1, nq - 1)
    sel_step(n_below, True)
    prepare_cmp(qn_ref, i_next, 1 - slot)
    prepare_sel(i_next, 1 - slot)

    w0 = pl.multiple_of(jnp.maximum(i - WINDOW // Q_BLOCK, 0) * Q_BLOCK, Q_BLOCK)
    kw = wink_ref[pl.ds(w0, WIN_KEYS), :]
    vw = winv_ref[pl.ds(w0, WIN_KEYS), :]

    def win_mask(rr):
        dist = (qpos0(i, rr) - w0) - rel(rr, WIN_KEYS)
        return pltpu.bitcast(dist, jnp.uint32) < WINDOW

    wblocks = [slice(r * WIN_ROW_BLOCK, (r + 1) * WIN_ROW_BLOCK) for r in range(nrow // WIN_ROW_BLOCK)]
    for rows, s in _one_ahead(wblocks, lambda rr: _nt(q2_ref[slot, rr, 0:LANES], kw)):
        ow_ref[rows, :] = _softmax_pv_rows(s, vw, win_mask, rows.start)

    gn = gn_ref[...]
    for c in range(N_Q_HEADS):
        h = c // GROUP
        rows = heads[c]
        o_s = acc_ref[rows, :] / jnp.maximum(l_ref[rows, :], 1e-30)
        o = (gn[:, 3 * c:3 * c + 1] * oc_ref[slot, rows, :] + gn[:, 3 * c + 1:3 * c + 2] * o_s
             + gn[:, 3 * c + 2:3 * c + 3] * ow_ref[rows, :])
        o_ref[:, c * HEAD_DIM:(c + 1) * HEAD_DIM] = o[:, h * HEAD_DIM:(h + 1) * HEAD_DIM]


def _q_placement():
    p = np.zeros((D_ATTN, N_Q_HEADS * LANES), np.float32)
    for c in range(N_Q_HEADS):
        h = c // GROUP
        for d in range(HEAD_DIM):
            p[c * HEAD_DIM + d, c * LANES + h * HEAD_DIM + d] = 1.0
    return jnp.asarray(p, BF16)


def _attn_prompt(q, gn, selk, selv, wink, winv, kcvc, batch, seq_len):
    nq = seq_len // Q_BLOCK
    n_c = kcvc.shape[1]
    assert seq_len % SEL_TK == 0 and seq_len // SEL_BLOCK <= LANES and seq_len >= WIN_KEYS and n_c % LANES == 0
    csel = jnp.asarray(_cmp_to_sel(n_c, n_c - 1, LANES), BF16)
    nrow = N_Q_HEADS * Q_BLOCK
    tile = lambda n: pl.BlockSpec((Q_BLOCK, n), lambda b, i: (b * nq + i, 0))
    next_tile = pl.BlockSpec((Q_BLOCK, D_ATTN), lambda b, i: (b * nq + jnp.minimum(i + 1, nq - 1), 0))
    seq = lambda n: pl.BlockSpec((seq_len, n), lambda b, i: (b, 0))
    rows_f32 = pltpu.VMEM((nrow, LANES), F32)
    return pl.pallas_call(
        functools.partial(_attn_prompt_kernel, nq=nq),
        grid=(batch, nq),
        in_specs=[tile(D_ATTN), next_tile, tile(LANES), seq(2 * LANES), seq(LANES), seq(LANES), seq(LANES),
                  pl.BlockSpec((1, n_c, 256), lambda b, i: (b, 0, 0)),
                  _full(csel.shape), _full((D_ATTN, N_Q_HEADS * LANES))],
        out_specs=tile(D_ATTN),
        out_shape=jax.ShapeDtypeStruct((batch * seq_len, D_ATTN), F32),
        scratch_shapes=[pltpu.VMEM((2, nrow, 2 * LANES), BF16), rows_f32, rows_f32, rows_f32,
                        pltpu.VMEM((2, nrow, LANES), F32), rows_f32,
                        pltpu.VMEM((N_KV_HEADS * Q_BLOCK, n_c), F32), pltpu.VMEM((ROW_BLOCK, SEL_TK), F32)],
        compiler_params=_cparams(("parallel", "arbitrary")),
        name="attn_prompt",
    )(q, q, gn, selk, selv, wink, winv, kcvc, csel, _q_placement())


SAMPLE_PAGES_PER_STEP = 32
SAMPLE_TK = 512


def _attn_sample_kernel(pt_ref, q_ref, gn_ref, kcvc_ref, csel_ref, pq_ref, oh_ref, selk_new_ref, selv_new_ref,
                        wink_ref, winv_ref, pages_ref, o_ref,
                        buf_ref, sem_ref, q1_ref, bias_ref, oc_ref, m_ref, l_ref, acc_ref, ow_ref,
                        *, n_groups, gp, past_len, ls):
    b = pl.program_id(0)
    j = pl.program_id(1)
    nrow = N_Q_HEADS * ls
    tk = gp * 128
    blocks_per_step = tk // SEL_BLOCK

    def copies(jj, slot):
        return [pltpu.make_async_copy(pages_ref.at[pt_ref[b, jj * gp + p], pl.ds(2, 2)],
                                      buf_ref.at[slot, p], sem_ref.at[slot])
                for p in range(gp)]

    @pl.when(j == 0)
    def _():
        for c in copies(0, 0):
            c.start()

    slot = j % 2

    @pl.when(j + 1 < n_groups)
    def _():
        for c in copies(j + 1, 1 - slot):
            c.start()

    tok = lax.broadcasted_iota(jnp.int32, (nrow, 1), 0) % ls
    qpos = past_len + tok

    @pl.when(j == 0)
    def _():
        qall = _dot(q_ref[0], pq_ref[...]).astype(BF16)
        q1 = jnp.concatenate([qall[:, c * LANES:(c + 1) * LANES] for c in range(N_Q_HEADS)], axis=0)
        q1_ref[...] = q1
        kc = kcvc_ref[0, :, 0:LANES]
        vc = kcvc_ref[0, :, LANES:2 * LANES]
        n_c = kc.shape[0]
        cmp_end = lax.broadcasted_iota(jnp.int32, (1, n_c), 1) * CMP_STRIDE + (CMP_BLOCK - 1)
        p_c = _softmax2_rows(_nt(q1, kc), cmp_end <= qpos)
        oc_ref[...] = _dot(p_c.astype(BF16), vc)
        hrows = GROUP * ls
        psum = jnp.concatenate(
            [sum(p_c[h * hrows + g * ls:h * hrows + (g + 1) * ls] for g in range(GROUP))
             for h in range(N_KV_HEADS)], axis=0)
        imp_t = _split_dot(psum, csel_ref[...]).T
        blk_t = lax.broadcasted_iota(jnp.int32, imp_t.shape, 0)
        qpos_t = past_len + lax.broadcasted_iota(jnp.int32, imp_t.shape, 1) % ls
        forced_t = jnp.logical_or(blk_t == 0, blk_t == qpos_t // SEL_BLOCK)
        sel_t = _topk_mask_t(imp_t, forced_t, blk_t * SEL_BLOCK <= qpos_t, blk_t)
        bias = jnp.where(sel_t > 0.0, 0.0, NEG).T.astype(BF16)
        for h in range(N_KV_HEADS):
            for g in range(GROUP):
                r0 = (h * GROUP + g) * ls
                bias_ref[r0:r0 + ls, :] = bias[h * ls:(h + 1) * ls]
        kpos_new = past_len + lax.broadcasted_iota(jnp.int32, (1, ls), 1)
        s = jnp.where(kpos_new <= qpos, _nt(q1, selk_new_ref[0]), NEG)
        m0 = jnp.max(s, axis=-1, keepdims=True)
        p = jnp.exp2(s - m0)
        m_ref[...] = jnp.broadcast_to(m0, m_ref.shape)
        l_ref[...] = jnp.broadcast_to(jnp.sum(p, axis=-1, keepdims=True), l_ref.shape)
        acc_ref[...] = _dot(p.astype(BF16), selv_new_ref[0])
        n_w = wink_ref.shape[1]
        kpos_w = past_len + ls - n_w + lax.broadcasted_iota(jnp.int32, (1, n_w), 1)
        dist = qpos - kpos_w
        mask_w = jnp.logical_and(jnp.logical_and(kpos_w >= 0, dist >= 0), dist < WINDOW)
        p_w = _softmax2_rows(_nt(q1, wink_ref[0]), mask_w)
        ow_ref[...] = _dot(p_w.astype(BF16), winv_ref[0])

    for c in copies(j, slot):
        c.wait()

    b0 = pl.multiple_of((j * blocks_per_step // LANES) * LANES, LANES)
    q2 = jnp.concatenate([bias_ref[:, pl.ds(b0, LANES)], q1_ref[...]], axis=1)
    ppt = SAMPLE_TK // 128

    def tile_t(t, kv):
        return jnp.concatenate([buf_ref[slot, t * ppt + p, kv].reshape(LANES, 128) for p in range(ppt)],
                               axis=1).astype(BF16)

    def scores(t):
        return _dot(q2, jnp.concatenate([oh_ref[:, t * SAMPLE_TK:(t + 1) * SAMPLE_TK], tile_t(t, 0)], axis=0))

    for t, s in _one_ahead(list(range(gp // ppt)), scores):
        _flash_rows(m_ref, l_ref, acc_ref, slice(0, nrow), s, tile_t(t, 1), sub=nrow, v_transposed=True)

    @pl.when(j == n_groups - 1)
    def _():
        gn = gn_ref[0]
        for c in range(N_Q_HEADS):
            h = c // GROUP
            rows = slice(c * ls, (c + 1) * ls)
            o_s = acc_ref[rows, :] / jnp.maximum(l_ref[rows, :], 1e-30)
            o = (gn[:, 3 * c:3 * c + 1] * oc_ref[rows, :] + gn[:, 3 * c + 1:3 * c + 2] * o_s
                 + gn[:, 3 * c + 2:3 * c + 3] * ow_ref[rows, :])
            o_ref[0, :, c * HEAD_DIM:(c + 1) * HEAD_DIM] = o[:, h * HEAD_DIM:(h + 1) * HEAD_DIM]


def _attn_sample(page_table, q, gn, kcvc, selk_new, selv_new, wink, winv, pages, past_len, ls):
    bs, n_pages = page_table.shape
    gp = min(SAMPLE_PAGES_PER_STEP, n_pages)
    n_groups = n_pages // gp
    tk = gp * 128
    blocks_per_step = tk // SEL_BLOCK
    assert LANES % blocks_per_step == 0 and tk % SAMPLE_TK == 0
    n_c = kcvc.shape[1]
    n_sel = -(-(past_len + ls) // SEL_BLOCK)
    n_sel_pad = -(-n_sel // LANES) * LANES
    csel = jnp.asarray(_cmp_to_sel(n_c, n_c - 1, n_sel_pad), BF16)
    key_blk = (np.arange(tk) // SEL_BLOCK)[None, :]
    onehot = [jnp.asarray((key_blk + s * blocks_per_step) % LANES == np.arange(LANES)[:, None], BF16)
              for s in range(LANES // blocks_per_step)]
    onehot = jnp.stack(onehot)
    n_rep = LANES // blocks_per_step
    nrow = N_Q_HEADS * ls
    kern = functools.partial(_attn_sample_kernel, n_groups=n_groups, gp=gp, past_len=past_len, ls=ls)
    per_seq = lambda *shape: pl.BlockSpec((1,) + shape, lambda b, j, pt: (b,) + (0,) * len(shape))
    const = lambda shape: pl.BlockSpec(shape, lambda b, j, pt: (0,) * len(shape))
    return pl.pallas_call(
        kern,
        grid_spec=pltpu.PrefetchScalarGridSpec(
            num_scalar_prefetch=1,
            grid=(bs, n_groups),
            in_specs=[per_seq(ls, D_ATTN), per_seq(ls, LANES), per_seq(n_c, 256),
                      const(csel.shape), const((D_ATTN, N_Q_HEADS * LANES)),
                      pl.BlockSpec((None, LANES, tk), lambda b, j, pt: (j % n_rep, 0, 0)),
                      per_seq(ls, LANES), per_seq(ls, LANES),
                      per_seq(wink.shape[1], LANES), per_seq(winv.shape[1], LANES),
                      pl.BlockSpec(memory_space=pl.ANY)],
            out_specs=per_seq(ls, D_ATTN),
            scratch_shapes=[pltpu.VMEM((2, gp, 2, N_KV_HEADS, HEAD_DIM, 128), F32),
                            pltpu.SemaphoreType.DMA((2,)),
                            pltpu.VMEM((nrow, LANES), BF16),
                            pltpu.VMEM((nrow, n_sel_pad), BF16),
                            pltpu.VMEM((nrow, LANES), F32),
                            pltpu.VMEM((nrow, LANES), F32),
                            pltpu.VMEM((nrow, LANES), F32),
                            pltpu.VMEM((nrow, LANES), F32),
                            pltpu.VMEM((nrow, LANES), F32)]),
        out_shape=jax.ShapeDtypeStruct((bs, ls, D_ATTN), F32),
        compiler_params=_cparams(("arbitrary", "arbitrary")),
        name="attn_sample",
    )(page_table, q, gn, kcvc, csel, _q_placement(), onehot, selk_new, selv_new, wink, winv, pages)


def _cmul(ar, ai, br, bi):
    return ar * br - ai * bi, ar * bi + ai * br


def _ssm_weights(lam_re, lam_im, log_dt, b_re, b_im, c_re, c_im, tc):
    dt = jnp.exp(log_dt)[:, None]
    mag = jnp.exp(lam_re * dt)
    a_re = mag * jnp.cos(lam_im * dt)
    a_im = mag * jnp.sin(lam_im * dt)
    den = lam_re * lam_re + lam_im * lam_im
    f_re = ((a_re - 1.0) * lam_re + a_im * lam_im) / den
    f_im = (a_im * lam_re - (a_re - 1.0) * lam_im) / den
    bb_re = f_re[..., None] * b_re - f_im[..., None] * b_im
    bb_im = f_re[..., None] * b_im + f_im[..., None] * b_re
    pr, pi = jnp.ones_like(a_re)[None], jnp.zeros_like(a_re)[None]
    sr, si = a_re, a_im
    while pr.shape[0] < tc + 1:
        nr, ni = _cmul(pr, pi, sr[None], si[None])
        pr, pi = jnp.concatenate([pr, nr]), jnp.concatenate([pi, ni])
        sr, si = _cmul(sr, si, sr, si)
    pr, pi = pr[:tc + 1], pi[:tc + 1]
    cpr = c_re[None] * pr[:, :, None, :] - c_im[None] * pi[:, :, None, :]
    cpi = c_re[None] * pi[:, :, None, :] + c_im[None] * pr[:, :, None, :]
    kker = jnp.einsum('tgop,gpi->gtoi', cpr[:tc], bb_re) - jnp.einsum('tgop,gpi->gtoi', cpi[:tc], bb_im)
    g = a_re.shape[0]
    nset = g // SSM_SET
    half = SSM_SET * SSM_STATE

    lag = np.arange(tc)[None, :] - np.arange(tc)[:, None]
    kpad = jnp.concatenate([kker, jnp.zeros_like(kker[:, :1])], axis=1)
    toep = kpad[:, np.where(lag >= 0, lag, tc)]
    t6 = jnp.transpose(toep, (0, 1, 4, 2, 3)).reshape(nset, SSM_SET, tc, SSM_GROUP, tc, SSM_GROUP)
    toep_c = jnp.transpose(t6, (0, 4, 2, 1, 3, 5)).reshape(nset, tc, tc * LANES, SSM_GROUP)
    qr, qi = pr[:tc][::-1], pi[:tc][::-1]
    n_re = qr[..., None] * bb_re[None] - qi[..., None] * bb_im[None]
    n_im = qr[..., None] * bb_im[None] + qi[..., None] * bb_re[None]
    n6 = jnp.transpose(jnp.stack([n_re, n_im], axis=2), (1, 0, 4, 2, 3))
    n6 = n6.reshape(nset, SSM_SET, tc, SSM_GROUP, 2, SSM_STATE)
    n_c = jnp.transpose(n6, (0, 4, 2, 1, 3, 5)).reshape(nset, 2, tc * LANES, SSM_STATE)
    m6 = jnp.transpose(jnp.stack([cpr[1:tc + 1], -cpi[1:tc + 1]], axis=3), (1, 3, 4, 0, 2))
    m6 = m6.reshape(nset, SSM_SET, 2, SSM_STATE, tc, SSM_GROUP)
    m_c = jnp.transpose(m6, (0, 4, 2, 1, 3, 5)).reshape(nset, tc, 2 * half, SSM_GROUP)
    return (toep_c.astype(BF16), n_c.astype(BF16), m_c.astype(BF16),
            pr[tc].reshape(1, nset * half), pi[tc].reshape(1, nset * half))


def _expand_blockdiag(c_ref, w_scr, rows_per_group):
    _, nblk, rows, unit = c_ref.shape
    width = SSM_SET * unit
    rep = (lax.broadcasted_iota(jnp.int32, (unit, width), 1) % unit
           == lax.broadcasted_iota(jnp.int32, (unit, width), 0))
    rep = jnp.where(rep, 1.0, 0.0).astype(BF16)
    row_group = (lax.broadcasted_iota(jnp.int32, (rows, width), 0) // rows_per_group) % SSM_SET
    mask = row_group == lax.broadcasted_iota(jnp.int32, (rows, width), 1) // unit
    for b in range(nblk):
        w_scr[:, b * width:(b + 1) * width] = jnp.where(mask, _dot(c_ref[0, b], rep), 0.0).astype(BF16)


def _chunk_rows(u_ref, tc):
    n = u_ref.shape[0] // tc
    return jnp.concatenate([u_ref[pl.ds(s, n, stride=tc), :] for s in range(tc)], axis=1)


def _ssm_state_kernel(u_ref, n_ref, s_ref, wn_scr, *, tc):
    @pl.when(pl.program_id(1) == 0)
    def _():
        _expand_blockdiag(n_ref, wn_scr, SSM_GROUP)

    s_ref[...] = _dot(_chunk_rows(u_ref, tc).astype(BF16), wn_scr[...])


def _ssm_carry_kernel(s_ref, h0_ref, are_ref, aim_ref, hs_ref, f_ref, h_scr, *, batch, nset):
    half = SSM_SET * SSM_STATE

    @pl.when(pl.program_id(0) == 0)
    def _():
        h_scr[...] = h0_ref[...]

    def body(k, h):
        sk = jnp.concatenate([s_ref[b, pl.ds(k, 1), :] for b in range(batch)], axis=0)
        for b in range(batch):
            hs_ref[b, pl.ds(k, 1), :] = h[b:b + 1]
        out = []
        for j in range(nset):
            c0 = 2 * j * half
            re, im = h[:, c0:c0 + half], h[:, c0 + half:c0 + 2 * half]
            ar, ai = are_ref[:, j * half:(j + 1) * half], aim_ref[:, j * half:(j + 1) * half]
            out.append(ar * re - ai * im + sk[:, c0:c0 + half])
            out.append(ar * im + ai * re + sk[:, c0 + half:c0 + 2 * half])
        return jnp.concatenate(out, axis=1)

    h = lax.fori_loop(0, s_ref.shape[1], body, h_scr[...])
    h_scr[...] = h
    f_ref[...] = h


def _ssm_out_kernel(u_ref, t_ref, m_ref, h_ref, y_ref, wt_scr, wm_scr, *, tc):
    @pl.when(pl.program_id(1) == 0)
    def _():
        _expand_blockdiag(t_ref, wt_scr, SSM_GROUP)
        _expand_blockdiag(m_ref, wm_scr, SSM_STATE)

    y = _dot(_chunk_rows(u_ref, tc).astype(BF16), wt_scr[...]) + _dot(h_ref[...].astype(BF16), wm_scr[...])
    n = y.shape[0]
    for s in range(tc):
        y_ref[pl.ds(s, n, stride=tc), :] = y[:, s * LANES:(s + 1) * LANES]


def _ssm(u2d, batch, seq_len, tc, h0_re, h0_im, ssm_w):
    toep, nmat, mmat, at_re, at_im = ssm_w
    nset = toep.shape[0]
    half = SSM_SET * SSM_STATE
    ncol = 2 * nset * half
    t_tok = batch * seq_len
    nck = seq_len // tc
    tm = min(2048, t_tok)
    rt = tm // tc
    grid = (nset, t_tok // tm)
    u_spec = pl.BlockSpec((tm, LANES), lambda j, i: (i, j))
    w_spec = lambda w: pl.BlockSpec((1,) + w.shape[1:], lambda j, i: (j, 0, 0, 0))
    st_spec = pl.BlockSpec((rt, 2 * half), lambda j, i: (i, j))
    kk = tc * LANES
    s = pl.pallas_call(
        functools.partial(_ssm_state_kernel, tc=tc),
        grid=grid,
        in_specs=[u_spec, w_spec(nmat)],
        out_specs=st_spec,
        out_shape=jax.ShapeDtypeStruct((batch * nck, ncol), F32),
        scratch_shapes=[pltpu.VMEM((kk, 2 * half), BF16)],
        compiler_params=_cparams(("parallel", "arbitrary")),
        name="ssm_state",
    )(u2d, nmat)
    kb = min(128, nck)
    h0 = jnp.stack([h0_re.reshape(batch, nset, half), h0_im.reshape(batch, nset, half)], axis=2).reshape(batch, ncol)
    seq3 = pl.BlockSpec((batch, kb, ncol), lambda c: (0, c, 0))
    hs, f = pl.pallas_call(
        functools.partial(_ssm_carry_kernel, batch=batch, nset=nset),
        grid=(nck // kb,),
        in_specs=[seq3, _full((batch, ncol)), _full(at_re.shape), _full(at_im.shape)],
        out_specs=[seq3, _full((batch, ncol))],
        out_shape=[jax.ShapeDtypeStruct((batch, nck, ncol), F32), jax.ShapeDtypeStruct((batch, ncol), F32)],
        scratch_shapes=[pltpu.VMEM((batch, ncol), F32)],
        compiler_params=_cparams(("arbitrary",)),
        name="ssm_carry",
    )(s.reshape(batch, nck, ncol), h0, at_re, at_im)
    y = pl.pallas_call(
        functools.partial(_ssm_out_kernel, tc=tc),
        grid=grid,
        in_specs=[u_spec, w_spec(toep), w_spec(mmat), st_spec],
        out_specs=u_spec,
        out_shape=jax.ShapeDtypeStruct(u2d.shape, F32),
        scratch_shapes=[pltpu.VMEM((kk, kk), BF16), pltpu.VMEM((2 * half, kk), BF16)],
        compiler_params=_cparams(("parallel", "arbitrary")),
        name="ssm_out",
    )(u2d, toep, mmat, hs.reshape(batch * nck, ncol))
    f = f.reshape(batch, nset, 2, SSM_SET, SSM_STATE)
    g = nset * SSM_SET
    return y, f[:, :, 0].reshape(batch, g, SSM_STATE), f[:, :, 1].reshape(batch, g, SSM_STATE)


def _post_kernel(x_ref, oa_ref, ys_ref, u_ref, g_ref, fg_ref, d_ref, bglu_ref,
                 wza_ref, wzs_ref, wgm_ref, wglu_ref, wla_ref, wls_ref, wo_ref, out_ref):
    x = x_ref[...]
    d = x.shape[1]
    xn = x * lax.rsqrt(jnp.mean(x * x, axis=-1, keepdims=True) + RMS_EPS) * g_ref[...]
    xb = xn.astype(BF16)
    z_a = _dot(xb, wza_ref[...])
    z_s = _dot(xb, wzs_ref[...])
    gm = jax.nn.sigmoid(_dot(xb, wgm_ref[...]))
    branch_a = _dot((oa_ref[...] * jax.nn.silu(z_a)).astype(BF16), wla_ref[...])
    y = jax.nn.gelu(ys_ref[...] + d_ref[...] * u_ref[...])
    y = y * jax.nn.sigmoid(_dot(y.astype(BF16), wglu_ref[...]) + bglu_ref[...])
    branch_b = _dot((y * jax.nn.silu(z_s)).astype(BF16), wls_ref[...])
    merged = gm[:, :d] * branch_a + gm[:, d:] * branch_b
    r = x + _dot(merged.astype(BF16), wo_ref[...])
    out_ref[...] = r * lax.rsqrt(jnp.mean(r * r, axis=-1, keepdims=True) + RMS_EPS) * fg_ref[...]


def _post(x2d, o_attn, y_ssm, u, norm_g, final_g, ssm_d, b_glu, wza, wzs, wgm, wglu, wla, wls, wo):
    t, d = x2d.shape
    tm = min(256, t)
    row = lambda n: pl.BlockSpec((tm, n), lambda i: (i, 0))
    ws = [wza, wzs, wgm, wglu, wla, wls, wo]
    vecs = [norm_g.reshape(1, d), final_g.reshape(1, d), ssm_d.reshape(1, -1), b_glu.reshape(1, -1)]
    return pl.pallas_call(
        _post_kernel,
        grid=(t // tm,),
        in_specs=[row(d), row(o_attn.shape[1]), row(y_ssm.shape[1]), row(u.shape[1])]
                 + [_full(v.shape) for v in vecs] + [_full(w.shape) for w in ws],
        out_specs=row(d),
        out_shape=jax.ShapeDtypeStruct((t, d), F32),
        compiler_params=_cparams(("parallel",)),
        name="post",
    )(x2d, o_attn, y_ssm, u, *vecs, *ws)


SSM_CHUNK = 8


def kernel(x_prompt, x_sample, cache_kv, cache_win_kv, state_ssm_re, state_ssm_im, page_table, norm_g, w_in, cmp_pe, cmp_w1, cmp_b1, cmp_w2, ssm_lam_re, ssm_lam_im, ssm_log_dt, ssm_b_re, ssm_b_im, ssm_c_re, ssm_c_im, ssm_d, w_glu, b_glu, w_lift_attn, w_lift_ssm, w_out, final_g):
    depth = norm_g.shape[0]
    assert depth == 1
    l = 0
    bp, lp, d = x_prompt.shape
    bs, ls, _ = x_sample.shape
    n_pages, page = page_table.shape[1], cache_kv.shape[2]
    past_len = n_pages * page
    w_buf = cache_win_kv.shape[2]
    d_ssm = ssm_d.shape[1]
    n_grp = d_ssm // SSM_GROUP
    assert page == 128 and w_buf == WINDOW and lp % Q_BLOCK == 0 and lp >= WINDOW

    splits = (D_ATTN, N_KV_SLOTS * N_KV_HEADS * HEAD_DIM, 3 * N_Q_HEADS, D_ATTN, d_ssm, d_ssm, 2 * d)
    offs = np.concatenate([[0], np.cumsum(splits)])
    wb = w_in[l].astype(BF16)
    wq, wkv, wg, wza, wu, wzs, wgm = [wb[:, offs[k]:offs[k + 1]] for k in range(7)]
    wg = jnp.pad(wg, ((0, 0), (0, LANES - wg.shape[1])))
    cw1, cb, cw2 = _compress_weights(cmp_pe[l], cmp_w1[l], cmp_b1[l], cmp_w2[l])
    ssm_p = (ssm_lam_re[l], ssm_lam_im[l], ssm_log_dt[l], ssm_b_re[l], ssm_b_im[l], ssm_c_re[l], ssm_c_im[l])
    post_w = (norm_g[l], final_g, ssm_d[l], b_glu[l], wza, wzs, wgm, w_glu[l].astype(BF16),
              w_lift_attn[l].astype(BF16), w_lift_ssm[l].astype(BF16), w_out[l].astype(BF16))

    xp = x_prompt.reshape(bp * lp, d)
    q, kv_t, win_t, cmp_rows, selk, selv, wink, winv, gn, u = _pre(xp, norm_g[l], wq, wkv, wg, wu, lp, True)
    ident = jnp.arange(bp * (lp // 128), dtype=jnp.int32).reshape(bp, lp // 128)
    kcvc = _compress(ident, cmp_rows.reshape(bp * lp // 128, 128, 2 * LANES), cw1, cb, cw2)
    o_attn = _attn_prompt(q, gn, selk, selv, wink, winv, kcvc, bp, lp)
    assert lp % SSM_CHUNK == 0 and ls == SSM_CHUNK
    ssm_w = _ssm_weights(*ssm_p, SSM_CHUNK)
    h0 = jnp.zeros((bp, n_grp, SSM_STATE), F32)
    y_ssm, hr_p, hi_p = _ssm(u, bp, lp, SSM_CHUNK, h0, h0, ssm_w)
    y_prompt = _post(xp, o_attn, y_ssm, u, *post_w).reshape(bp, lp, d)
    kv_prompt = jnp.transpose(kv_t.reshape(bp, 4, N_KV_HEADS, HEAD_DIM, lp), (0, 4, 1, 2, 3))[None]
    win_t = win_t.reshape(bp, 2, N_KV_HEADS, HEAD_DIM, lp)[..., lp - WINDOW:]
    win_prompt = jnp.transpose(win_t, (0, 4, 1, 2, 3))[None]

    xs = x_sample.reshape(bs * ls, d)
    q, kv, win, _, selk, selv, wink, winv, gn, u = _pre(xs, norm_g[l], wq, wkv, wg, wu, ls, False)
    pages = jnp.transpose(cache_kv[l], (0, 2, 3, 4, 1))
    kcvc = _compress(page_table, pages, cw1, cb, cw2)
    cwin = cache_win_kv[l].reshape(bs, w_buf, 2 * LANES)
    wk_all = jnp.concatenate([cwin[:, :, :LANES].astype(BF16), wink.reshape(bs, ls, LANES)], axis=1)
    wv_all = jnp.concatenate([cwin[:, :, LANES:].astype(BF16), winv.reshape(bs, ls, LANES)], axis=1)
    o_attn = _attn_sample(page_table, q.reshape(bs, ls, D_ATTN), gn.reshape(bs, ls, LANES), kcvc,
                          selk[:, :LANES].reshape(bs, ls, LANES), selv.reshape(bs, ls, LANES),
                          wk_all, wv_all, pages, past_len, ls)
    y_ssm, hr_s, hi_s = _ssm(u, bs, ls, SSM_CHUNK, state_ssm_re[l], state_ssm_im[l], ssm_w)
    y_sample = _post(xs, o_attn.reshape(bs * ls, D_ATTN), y_ssm, u, *post_w).reshape(bs, ls, d)
    kv_sample = kv.reshape(1, bs, ls, 4, N_KV_HEADS, HEAD_DIM)
    win_new = win.reshape(bs, ls, 2, N_KV_HEADS, HEAD_DIM)
    win_sample = jnp.concatenate([cache_win_kv[l], win_new], axis=1)[None, :, ls:]

    return (y_prompt, y_sample, kv_prompt, win_prompt, hr_p[None], hi_p[None],
            kv_sample, win_sample, hr_s[None], hi_s[None])
```

```python
import functools
import math

import numpy as np
import jax
import jax.numpy as jnp
from jax import lax
from jax.experimental import pallas as pl
from jax.experimental.pallas import tpu as pltpu

F32 = jnp.float32
BF16 = jnp.bfloat16

HEAD_DIM = 64
N_KV_HEADS = 2
GROUP = 4
N_Q_HEADS = N_KV_HEADS * GROUP
D_ATTN = N_Q_HEADS * HEAD_DIM
N_KV_SLOTS = 6
CMP_BLOCK = 32
CMP_STRIDE = 16
SEL_BLOCK = 64
N_SEL = 16
WINDOW = 512
Q_BLOCK = 128
SSM_GROUP = 16
SSM_STATE = 64
SSM_SET = 8
RMS_EPS = 1e-6
LANES = 128
KV_ROW = 4 * N_KV_HEADS * HEAD_DIM
NEG = -1e30
LOG2E = math.log2(math.e)
VMEM_LIMIT = 56 * 1024 * 1024
TOKEN_TILE = 512


def _cparams(sem):
    return pltpu.CompilerParams(dimension_semantics=sem, vmem_limit_bytes=VMEM_LIMIT)


def _full(shape):
    nd = len(shape)
    return pl.BlockSpec(shape, lambda *_: (0,) * nd)


def _nt(a, b):
    return lax.dot_general(a, b, (((1,), (1,)), ((), ())), preferred_element_type=F32)


def _dot(a, b):
    return jnp.dot(a, b, preferred_element_type=F32)


def _pre_kernel(x_ref, g_ref, wq_ref, wkv_ref, wg_ref, wu_ref,
                q_ref, kv_ref, win_ref, cmp_ref, selk_ref, selv_ref, wink_ref, winv_ref, gn_ref, u_ref,
                *, seq_len, tm, transposed_kv):
    x = x_ref[...]
    xn = x * lax.rsqrt(jnp.mean(x * x, axis=-1, keepdims=True) + RMS_EPS) * g_ref[...]
    xb = xn.astype(BF16)
    q = _dot(xb, wq_ref[...])
    q_ref[...] = (q * (LOG2E * HEAD_DIM ** -0.5)).astype(BF16)
    kv = _dot(xb, wkv_ref[...])
    if transposed_kv:
        for c in range(4):
            kv_ref[c * LANES:(c + 1) * LANES, :] = kv[:, c * LANES:(c + 1) * LANES].T
        for c in range(2):
            win_ref[c * LANES:(c + 1) * LANES, :] = kv[:, KV_ROW + c * LANES:KV_ROW + (c + 1) * LANES].T
    else:
        kv_ref[...] = kv[:, :KV_ROW]
        win_ref[...] = kv[:, KV_ROW:]
    cmp_ref[...] = kv[:, 0:2 * LANES]
    pos = (pl.program_id(0) * tm + lax.broadcasted_iota(jnp.int32, (tm, LANES), 0)) % seq_len
    onehot = lax.broadcasted_iota(jnp.int32, (tm, LANES), 1) == (pos // SEL_BLOCK) % LANES
    selk_ref[:, :LANES] = kv[:, 256:384].astype(BF16)
    selk_ref[:, LANES:] = jnp.where(onehot, 1.0, 0.0).astype(BF16)
    selv_ref[...] = kv[:, 384:512].astype(BF16)
    wink_ref[...] = kv[:, 512:640].astype(BF16)
    winv_ref[...] = kv[:, 640:768].astype(BF16)
    gn_ref[...] = jax.nn.sigmoid(_dot(xb, wg_ref[...]))
    u_ref[...] = _dot(xb, wu_ref[...])


def _pre(x2d, norm_g, wq, wkv, wg, wu, seq_len, transposed_kv):
    t, d = x2d.shape
    tm = min(TOKEN_TILE, t)
    kern = functools.partial(_pre_kernel, seq_len=seq_len, tm=tm, transposed_kv=transposed_kv)
    row = lambda n: pl.BlockSpec((tm, n), lambda i: (i, 0))
    outs = [(D_ATTN, BF16), (KV_ROW, F32), (2 * LANES, F32), (2 * LANES, F32), (2 * LANES, BF16), (LANES, BF16),
            (LANES, BF16), (LANES, BF16), (LANES, F32), (wu.shape[1], F32)]
    out_specs = [row(n) for n, _ in outs]
    out_shape = [jax.ShapeDtypeStruct((t, n), dt) for n, dt in outs]
    if transposed_kv:
        per_seq = seq_len // tm
        for k in (1, 2):
            n = outs[k][0]
            out_specs[k] = pl.BlockSpec((None, n, tm), lambda i: (i // per_seq, 0, i % per_seq))
            out_shape[k] = jax.ShapeDtypeStruct((t // seq_len, n, seq_len), F32)
    return pl.pallas_call(
        kern,
        grid=(t // tm,),
        in_specs=[row(d), _full((1, d)), _full(wq.shape), _full(wkv.shape), _full(wg.shape), _full(wu.shape)],
        out_specs=out_specs,
        out_shape=out_shape,
        compiler_params=_cparams(("parallel",)),
        name="pre",
    )(x2d, norm_g.reshape(1, d), wq, wkv, wg, wu)


PAGES_PER_STEP = 32
CMP_PAGE_PARTS = 2


def _compress_kernel(pt_ref, pages_ref, w1_ref, b_ref, w2_ref, out_ref, buf_ref, sem_ref, h_ref, *tbuf_ref,
                     n_groups, gp, transposed):
    b = pl.program_id(0)
    j = pl.program_id(1)
    rows = gp * (128 // CMP_STRIDE)

    def copies(jj, slot):
        if transposed:
            return [pltpu.make_async_copy(pages_ref.at[pt_ref[b, jj * gp + p], pl.ds(0, 2)],
                                          tbuf_ref[0].at[slot, p], sem_ref.at[slot])
                    for p in range(gp)]
        return [pltpu.make_async_copy(pages_ref.at[pt_ref[b, jj * gp + p], :, pl.ds(half * LANES, LANES)],
                                      buf_ref.at[slot, half, p], sem_ref.at[slot])
                for p in range(gp) for half in range(2)]

    @pl.when(j == 0)
    def _():
        for c in copies(0, 0):
            c.start()

    slot = j % 2

    @pl.when(j + 1 < n_groups)
    def _():
        for c in copies(j + 1, 1 - slot):
            c.start()

    for c in copies(j, slot):
        c.wait()

    n_part = CMP_PAGE_PARTS if transposed and gp % CMP_PAGE_PARTS == 0 else 1
    pp = gp // n_part
    prows = rows // n_part
    rslot = 0 if transposed else slot
    for part in range(n_part):
        p0 = part * pp
        if transposed:
            for p in range(p0, p0 + pp):
                for half in range(2):
                    buf_ref[0, half, p] = tbuf_ref[0][slot, p, half].reshape(LANES, 128).T
        acc = jnp.zeros((prows, 512), F32)
        for jj in range(CMP_STRIDE):
            xj = [buf_ref[rslot, half, p0:p0 + pp, pl.ds(jj, 128 // CMP_STRIDE, stride=CMP_STRIDE), :]
                  .reshape(prows, LANES) for half in range(2)]
            xj = jnp.concatenate(xj, axis=1).astype(BF16)
            acc = acc + _dot(xj, w1_ref[jj])
        h_ref[pl.ds(pl.multiple_of(j * rows + part * prows, prows), prows), :] = acc

    @pl.when(j == n_groups - 1)
    def _():
        n = h_ref.shape[0]
        hb = pltpu.roll(h_ref[:, 256:512], n - 1, 0)
        hid = h_ref[:, 0:256] + hb + b_ref[...]
        act = jax.nn.gelu(hid).astype(BF16)
        out_ref[0] = _dot(act, w2_ref[...]).astype(BF16)


def _compress(page_table, pages, w1, b_eff, w2):
    n_seq, n_pages = page_table.shape
    gp = min(PAGES_PER_STEP, n_pages)
    n_groups = n_pages // gp
    n_chunks = n_pages * (128 // CMP_STRIDE)
    transposed = pages.ndim == 5
    kern = functools.partial(_compress_kernel, n_groups=n_groups, gp=gp, transposed=transposed)
    scratch = [pltpu.VMEM((1 if transposed else 2, 2, gp, 128, LANES), F32),
               pltpu.SemaphoreType.DMA((2,)),
               pltpu.VMEM((n_chunks, 512), F32)]
    if transposed:
        scratch.append(pltpu.VMEM((2, gp, 2, N_KV_HEADS, HEAD_DIM, 128), F32))
    return pl.pallas_call(
        kern,
        grid_spec=pltpu.PrefetchScalarGridSpec(
            num_scalar_prefetch=1,
            grid=(n_seq, n_groups),
            in_specs=[pl.BlockSpec(memory_space=pl.ANY),
                      pl.BlockSpec(w1.shape, lambda b, j, pt: (0, 0, 0)),
                      pl.BlockSpec(b_eff.shape, lambda b, j, pt: (0, 0)),
                      pl.BlockSpec(w2.shape, lambda b, j, pt: (0, 0))],
            out_specs=pl.BlockSpec((1, n_chunks, 256), lambda b, j, pt: (b, 0, 0)),
            scratch_shapes=scratch),
        out_shape=jax.ShapeDtypeStruct((n_seq, n_chunks, 256), BF16),
        compiler_params=_cparams(("arbitrary", "arbitrary")),
        name="compress",
    )(page_table, pages, w1, b_eff, w2)


def _compress_weights(cmp_pe, cmp_w1, cmp_b1, cmp_w2):
    w1 = cmp_w1.reshape(2, 2, CMP_STRIDE, HEAD_DIM, HEAD_DIM)
    eye = jnp.eye(N_KV_HEADS, dtype=F32)
    blk = jnp.einsum('vrjde,vw,hg->rjvhdwge', w1, jnp.eye(2, dtype=F32), eye)
    blk = blk.reshape(2, CMP_STRIDE, 256, 256)
    w1_all = jnp.concatenate([blk[0], blk[1]], axis=-1).astype(BF16)
    pe = cmp_pe.reshape(2, 2, CMP_STRIDE, HEAD_DIM)
    pe_bias = jnp.einsum('vrjd,vrjde->ve', pe, w1) + cmp_b1
    b_eff = jnp.broadcast_to(pe_bias[:, None, :], (2, N_KV_HEADS, HEAD_DIM)).reshape(1, 256)
    w2 = jnp.einsum('ved,vw,hg->vhewgd', cmp_w2, jnp.eye(2, dtype=F32), eye).reshape(256, 256).astype(BF16)
    return w1_all, b_eff, w2


def _cmp_to_sel(n_cmp_rows, n_valid, n_sel_pad):
    c0 = np.arange(n_cmp_rows)[:, None] * CMP_STRIDE
    s0 = np.arange(n_sel_pad)[None, :] * SEL_BLOCK
    shared = np.minimum(c0 + CMP_BLOCK, s0 + SEL_BLOCK) - np.maximum(c0, s0)
    m = np.clip(shared, 0, None).astype(np.float32) / CMP_BLOCK
    m[n_valid:] = 0.0
    return m


def _softmax2_rows(s, mask):
    s = jnp.where(mask, s, -jnp.inf)
    m = jnp.max(s, axis=-1, keepdims=True)
    m = jnp.where(jnp.isfinite(m), m, 0.0)
    e = jnp.exp2(s - m)
    return e / jnp.maximum(jnp.sum(e, axis=-1, keepdims=True), 1e-30)


def _split_dot(p, w):
    hi = p.astype(BF16)
    lo = (p - hi.astype(F32)).astype(BF16)
    return _dot(hi, w) + _dot(lo, w)


def _topk_mask_t(imp_t, forced_t, causal_t, blk_t):
    v = jnp.where(forced_t, jnp.inf, jnp.where(causal_t, imp_t, -jnp.inf))
    big = jnp.int32(imp_t.shape[0])
    sel = jnp.zeros(imp_t.shape, F32)
    for _ in range(N_SEL):
        mx = jnp.max(v, axis=0, keepdims=True)
        idx = jnp.min(jnp.where(v == mx, blk_t, big), axis=0, keepdims=True)
        pick = blk_t == idx
        sel = jnp.where(pick, jnp.maximum(sel, jnp.where(mx > -jnp.inf, 1.0, 0.0)), sel)
        v = jnp.where(pick, -jnp.inf, v)
    return sel


def _lane_groups(s):
    return [s[:, j * LANES:(j + 1) * LANES] for j in range(s.shape[1] // LANES)]


def _one_ahead(items, make):
    nxt = make(items[0])
    for n, item in enumerate(items):
        cur = nxt
        if n + 1 < len(items):
            nxt = make(items[n + 1])
        yield item, cur


def _flash_rows(m_ref, l_ref, acc_ref, rows, s, v, mask_fn=None, sub=64, v_transposed=False):
    r = s.shape[0]
    ps, alphas = [], []
    for u in range(r // sub):
        rr = slice(rows.start + u * sub, rows.start + (u + 1) * sub)
        su = s[u * sub:(u + 1) * sub]
        if mask_fn is not None:
            su = jnp.where(mask_fn(rr), su, NEG)
        sj = _lane_groups(su)
        m_prev = m_ref[rr, :]
        m_new = jnp.maximum(m_prev, jnp.max(functools.reduce(jnp.maximum, sj), axis=-1, keepdims=True))
        alpha = jnp.exp2(m_prev - m_new)
        pj = [jnp.exp2(x - m_new) for x in sj]
        l_ref[rr, :] = alpha * l_ref[rr, :] + jnp.sum(functools.reduce(jnp.add, pj), axis=-1, keepdims=True)
        m_ref[rr, :] = m_new
        ps.append(jnp.concatenate([x.astype(BF16) for x in pj], axis=1))
        alphas.append(alpha)
    p = jnp.concatenate(ps, axis=0)
    pv = _nt(p, v) if v_transposed else _dot(p, v)
    acc_ref[rows, :] = jnp.concatenate(alphas, axis=0) * acc_ref[rows, :] + pv


def _softmax_weights(s, mask_fn, row0, sub=64):
    out = []
    for u in range(s.shape[0] // sub):
        su = jnp.where(mask_fn(slice(row0 + u * sub, row0 + (u + 1) * sub)), s[u * sub:(u + 1) * sub], -jnp.inf)
        sj = _lane_groups(su)
        m = jnp.max(functools.reduce(jnp.maximum, sj), axis=-1, keepdims=True)
        m = jnp.where(jnp.isfinite(m), m, 0.0)
        ej = [jnp.exp2(x - m) for x in sj]
        inv = 1.0 / jnp.maximum(jnp.sum(functools.reduce(jnp.add, ej), axis=-1, keepdims=True), 1e-30)
        out.append(jnp.concatenate([e * inv for e in ej], axis=1))
    return jnp.concatenate(out, axis=0)


def _softmax_pv_rows(s, v, mask_fn, row0, sub=64):
    r = s.shape[0]
    ps, ls = [], []
    for u in range(r // sub):
        su = jnp.where(mask_fn(slice(row0 + u * sub, row0 + (u + 1) * sub)), s[u * sub:(u + 1) * sub], NEG)
        sj = _lane_groups(su)
        m = jnp.max(functools.reduce(jnp.maximum, sj), axis=-1, keepdims=True)
        pj = [jnp.exp2(x - m) for x in sj]
        ls.append(jnp.broadcast_to(jnp.sum(functools.reduce(jnp.add, pj), axis=-1, keepdims=True), (sub, LANES)))
        ps.append(jnp.concatenate([x.astype(BF16) for x in pj], axis=1))
    return _dot(jnp.concatenate(ps, axis=0), v) / jnp.maximum(jnp.concatenate(ls, axis=0), 1e-30)


SEL_TK = 512
SEL_SUB = 64
SEL_UNROLL = 4
ROW_BLOCK = 512
WIN_ROW_BLOCK = 512
WIN_KEYS = WINDOW + Q_BLOCK


def _attn_prompt_kernel(q_ref, qn_ref, gn_ref, selk_ref, selv_ref, wink_ref, winv_ref, kcvc_ref, csel_ref, pq_ref,
                        o_ref, q2_ref, m_ref, l_ref, acc_ref, oc_ref, ow_ref, ps_ref, s0_ref, *, nq):
    i = pl.program_id(1)
    nrow = N_Q_HEADS * Q_BLOCK
    slot = i % 2
    blocks = [slice(r * ROW_BLOCK, (r + 1) * ROW_BLOCK) for r in range(nrow // ROW_BLOCK)]
    heads = [slice(c * Q_BLOCK, (c + 1) * Q_BLOCK) for c in range(N_Q_HEADS)]

    def rel(rr, n, step=1):
        shape = (rr.stop - rr.start, n)
        return lax.broadcasted_iota(jnp.int32, shape, 1) * step - lax.broadcasted_iota(jnp.int32, shape, 0)

    def qpos0(iq, rr):
        return iq * Q_BLOCK + rr.start % Q_BLOCK

    def prepare_cmp(qt_ref, iq, sl):
        qall = _dot(qt_ref[...], pq_ref[...]).astype(BF16)
        for c in range(N_Q_HEADS):
            q2_ref[sl, c * Q_BLOCK:(c + 1) * Q_BLOCK, 0:LANES] = qall[:, c * LANES:(c + 1) * LANES]
        kc = kcvc_ref[0, :, 0:LANES]
        vc = kcvc_ref[0, :, LANES:2 * LANES]
        n_c = kc.shape[0]
        for rows, s in _one_ahead(heads, lambda rr: _nt(q2_ref[sl, rr, 0:LANES], kc)):
            c = rows.start // Q_BLOCK
            p_c = _softmax_weights(
                s, lambda rr: rel(rr, n_c, CMP_STRIDE) <= qpos0(iq, rr) - (CMP_BLOCK - 1), rows.start)
            oc_ref[sl, rows, :] = _dot(p_c.astype(BF16), vc)
            prow = slice((c // GROUP) * Q_BLOCK, (c // GROUP + 1) * Q_BLOCK)
            if c % GROUP == 0:
                ps_ref[prow, :] = p_c
            else:
                ps_ref[prow, :] += p_c

    def prepare_sel(iq, sl):
        imp_t = _split_dot(ps_ref[...], csel_ref[...]).T
        blk_t = lax.broadcasted_iota(jnp.int32, imp_t.shape, 0)
        qpos_t = iq * Q_BLOCK + lax.broadcasted_iota(jnp.int32, imp_t.shape, 1) % Q_BLOCK
        forced_t = jnp.logical_or(blk_t == 0, blk_t == qpos_t // SEL_BLOCK)
        sel_t = _topk_mask_t(imp_t, forced_t, blk_t * SEL_BLOCK <= qpos_t, blk_t)
        bias = jnp.where(sel_t > 0.0, 0.0, NEG).T.astype(BF16)
        for c in range(N_Q_HEADS):
            h = c // GROUP
            q2_ref[sl, c * Q_BLOCK:(c + 1) * Q_BLOCK, LANES:2 * LANES] = bias[h * Q_BLOCK:(h + 1) * Q_BLOCK]

    @pl.when(i == 0)
    def _():
        prepare_cmp(q_ref, 0, 0)
        prepare_sel(0, 0)

    m_ref[...] = jnp.full(m_ref.shape, -jnp.inf, F32)
    l_ref[...] = jnp.zeros(l_ref.shape, F32)
    acc_ref[...] = jnp.zeros(acc_ref.shape, F32)

    def scores(rr, t):
        return _nt(q2_ref[slot, rr, :], selk_ref[pl.ds(pl.multiple_of(t * SEL_TK, SEL_TK), SEL_TK), :])

    def sel_step(t, diagonal):
        k0 = pl.multiple_of(t * SEL_TK, SEL_TK)
        v2 = selv_ref[pl.ds(k0, SEL_TK), :]
        mask_fn = (lambda rr: rel(rr, SEL_TK) <= qpos0(i, rr) - k0) if diagonal else None
        pending = s0_ref[...]
        for n, rows in enumerate(blocks):
            s = pending
            if n + 1 < len(blocks):
                pending = scores(blocks[n + 1], t)
            elif not diagonal:
                s0_ref[...] = scores(blocks[0], t + 1)
            _flash_rows(m_ref, l_ref, acc_ref, rows, s, v2, mask_fn, sub=SEL_SUB)

    n_below = (i * Q_BLOCK) // SEL_TK
    s0_ref[...] = scores(blocks[0], 0)

    def body(tt, carry):
        for k in range(SEL_UNROLL):
            sel_step(SEL_UNROLL * tt + k, False)
        return carry

    def tail(t, carry):
        sel_step(t, False)
        return carry

    n_trips = n_below // SEL_UNROLL
    lax.fori_loop(0, n_trips, body, 0)
    lax.fori_loop(n_trips * SEL_UNROLL, n_below, tail, 0)

    i_next = jnp.minimum(i + 1, nq - 1)
    sel_step(n_below, True)
    prepare_cmp(qn_ref, i_next, 1 - slot)
    prepare_sel(i_next, 1 - slot)

    w0 = pl.multiple_of(jnp.maximum(i - WINDOW // Q_BLOCK, 0) * Q_BLOCK, Q_BLOCK)
    kw = wink_ref[pl.ds(w0, WIN_KEYS), :]
    vw = winv_ref[pl.ds(w0, WIN_KEYS), :]

    def win_mask(rr):
        dist = (qpos0(i, rr) - w0) - rel(rr, WIN_KEYS)
        return pltpu.bitcast(dist, jnp.uint32) < WINDOW

    wblocks = [slice(r * WIN_ROW_BLOCK, (r + 1) * WIN_ROW_BLOCK) for r in range(nrow // WIN_ROW_BLOCK)]
    for rows, s in _one_ahead(wblocks, lambda rr: _nt(q2_ref[slot, rr, 0:LANES], kw)):
        ow_ref[rows, :] = _softmax_pv_rows(s, vw, win_mask, rows.start)

    gn = gn_ref[...]
    for c in range(N_Q_HEADS):
        h = c // GROUP
        rows = heads[c]
        o_s = acc_ref[rows, :] / jnp.maximum(l_ref[rows, :], 1e-30)
        o = (gn[:, 3 * c:3 * c + 1] * oc_ref[slot, rows, :] + gn[:, 3 * c + 1:3 * c + 2] * o_s
             + gn[:, 3 * c + 2:3 * c + 3] * ow_ref[rows, :])
        o_ref[:, c * HEAD_DIM:(c + 1) * HEAD_DIM] = o[:, h * HEAD_DIM:(h + 1) * HEAD_DIM]


def _q_placement():
    p = np.zeros((D_ATTN, N_Q_HEADS * LANES), np.float32)
    for c in range(N_Q_HEADS):
        h = c // GROUP
        for d in range(HEAD_DIM):
            p[c * HEAD_DIM + d, c * LANES + h * HEAD_DIM + d] = 1.0
    return jnp.asarray(p, BF16)


def _attn_prompt(q, gn, selk, selv, wink, winv, kcvc, batch, seq_len):
    nq = seq_len // Q_BLOCK
    n_c = kcvc.shape[1]
    assert seq_len % SEL_TK == 0 and seq_len // SEL_BLOCK <= LANES and seq_len >= WIN_KEYS and n_c % LANES == 0
    csel = jnp.asarray(_cmp_to_sel(n_c, n_c - 1, LANES), BF16)
    nrow = N_Q_HEADS * Q_BLOCK
    tile = lambda n: pl.BlockSpec((Q_BLOCK, n), lambda b, i: (b * nq + i, 0))
    next_tile = pl.BlockSpec((Q_BLOCK, D_ATTN), lambda b, i: (b * nq + jnp.minimum(i + 1, nq - 1), 0))
    seq = lambda n: pl.BlockSpec((seq_len, n), lambda b, i: (b, 0))
    rows_f32 = pltpu.VMEM((nrow, LANES), F32)
    return pl.pallas_call(
        functools.partial(_attn_prompt_kernel, nq=nq),
        grid=(batch, nq),
        in_specs=[tile(D_ATTN), next_tile, tile(LANES), seq(2 * LANES), seq(LANES), seq(LANES), seq(LANES),
                  pl.BlockSpec((1, n_c, 256), lambda b, i: (b, 0, 0)),
                  _full(csel.shape), _full((D_ATTN, N_Q_HEADS * LANES))],
        out_specs=tile(D_ATTN),
        out_shape=jax.ShapeDtypeStruct((batch * seq_len, D_ATTN), F32),
        scratch_shapes=[pltpu.VMEM((2, nrow, 2 * LANES), BF16), rows_f32, rows_f32, rows_f32,
                        pltpu.VMEM((2, nrow, LANES), F32), rows_f32,
                        pltpu.VMEM((N_KV_HEADS * Q_BLOCK, n_c), F32), pltpu.VMEM((ROW_BLOCK, SEL_TK), F32)],
        compiler_params=_cparams(("parallel", "arbitrary")),
        name="attn_prompt",
    )(q, q, gn, selk, selv, wink, winv, kcvc, csel, _q_placement())


SAMPLE_PAGES_PER_STEP = 32
SAMPLE_TK = 512


def _attn_sample_kernel(pt_ref, q_ref, gn_ref, kcvc_ref, csel_ref, pq_ref, oh_ref, selk_new_ref, selv_new_ref,
                        wink_ref, winv_ref, pages_ref, o_ref,
                        buf_ref, sem_ref, q1_ref, bias_ref, oc_ref, m_ref, l_ref, acc_ref, ow_ref,
                        *, n_groups, gp, past_len, ls):
    b = pl.program_id(0)
    j = pl.program_id(1)
    nrow = N_Q_HEADS * ls
    tk = gp * 128
    blocks_per_step = tk // SEL_BLOCK

    def copies(jj, slot):
        return [pltpu.make_async_copy(pages_ref.at[pt_ref[b, jj * gp + p], pl.ds(2, 2)],
                                      buf_ref.at[slot, p], sem_ref.at[slot])
                for p in range(gp)]

    @pl.when(j == 0)
    def _():
        for c in copies(0, 0):
            c.start()

    slot = j % 2

    @pl.when(j + 1 < n_groups)
    def _():
        for c in copies(j + 1, 1 - slot):
            c.start()

    tok = lax.broadcasted_iota(jnp.int32, (nrow, 1), 0) % ls
    qpos = past_len + tok

    @pl.when(j == 0)
    def _():
        qall = _dot(q_ref[0], pq_ref[...]).astype(BF16)
        q1 = jnp.concatenate([qall[:, c * LANES:(c + 1) * LANES] for c in range(N_Q_HEADS)], axis=0)
        q1_ref[...] = q1
        kc = kcvc_ref[0, :, 0:LANES]
        vc = kcvc_ref[0, :, LANES:2 * LANES]
        n_c = kc.shape[0]
        cmp_end = lax.broadcasted_iota(jnp.int32, (1, n_c), 1) * CMP_STRIDE + (CMP_BLOCK - 1)
        p_c = _softmax2_rows(_nt(q1, kc), cmp_end <= qpos)
        oc_ref[...] = _dot(p_c.astype(BF16), vc)
        hrows = GROUP * ls
        psum = jnp.concatenate(
            [sum(p_c[h * hrows + g * ls:h * hrows + (g + 1) * ls] for g in range(GROUP))
             for h in range(N_KV_HEADS)], axis=0)
        imp_t = _split_dot(psum, csel_ref[...]).T
        blk_t = lax.broadcasted_iota(jnp.int32, imp_t.shape, 0)
        qpos_t = past_len + lax.broadcasted_iota(jnp.int32, imp_t.shape, 1) % ls
        forced_t = jnp.logical_or(blk_t == 0, blk_t == qpos_t // SEL_BLOCK)
        sel_t = _topk_mask_t(imp_t, forced_t, blk_t * SEL_BLOCK <= qpos_t, blk_t)
        bias = jnp.where(sel_t > 0.0, 0.0, NEG).T.astype(BF16)
        for h in range(N_KV_HEADS):
            for g in range(GROUP):
                r0 = (h * GROUP + g) * ls
                bias_ref[r0:r0 + ls, :] = bias[h * ls:(h + 1) * ls]
        kpos_new = past_len + lax.broadcasted_iota(jnp.int32, (1, ls), 1)
        s = jnp.where(kpos_new <= qpos, _nt(q1, selk_new_ref[0]), NEG)
        m0 = jnp.max(s, axis=-1, keepdims=True)
        p = jnp.exp2(s - m0)
        m_ref[...] = jnp.broadcast_to(m0, m_ref.shape)
        l_ref[...] = jnp.broadcast_to(jnp.sum(p, axis=-1, keepdims=True), l_ref.shape)
        acc_ref[...] = _dot(p.astype(BF16), selv_new_ref[0])
        n_w = wink_ref.shape[1]
        kpos_w = past_len + ls - n_w + lax.broadcasted_iota(jnp.int32, (1, n_w), 1)
        dist = qpos - kpos_w
        mask_w = jnp.logical_and(jnp.logical_and(kpos_w >= 0, dist >= 0), dist < WINDOW)
        p_w = _softmax2_rows(_nt(q1, wink_ref[0]), mask_w)
        ow_ref[...] = _dot(p_w.astype(BF16), winv_ref[0])

    for c in copies(j, slot):
        c.wait()

    b0 = pl.multiple_of((j * blocks_per_step // LANES) * LANES, LANES)
    q2 = jnp.concatenate([bias_ref[:, pl.ds(b0, LANES)], q1_ref[...]], axis=1)
    ppt = SAMPLE_TK // 128

    def tile_t(t, kv):
        return jnp.concatenate([buf_ref[slot, t * ppt + p, kv].reshape(LANES, 128) for p in range(ppt)],
                               axis=1).astype(BF16)

    def scores(t):
        return _dot(q2, jnp.concatenate([oh_ref[:, t * SAMPLE_TK:(t + 1) * SAMPLE_TK], tile_t(t, 0)], axis=0))

    for t, s in _one_ahead(list(range(gp // ppt)), scores):
        _flash_rows(m_ref, l_ref, acc_ref, slice(0, nrow), s, tile_t(t, 1), sub=nrow, v_transposed=True)

    @pl.when(j == n_groups - 1)
    def _():
        gn = gn_ref[0]
        for c in range(N_Q_HEADS):
            h = c // GROUP
            rows = slice(c * ls, (c + 1) * ls)
            o_s = acc_ref[rows, :] / jnp.maximum(l_ref[rows, :], 1e-30)
            o = (gn[:, 3 * c:3 * c + 1] * oc_ref[rows, :] + gn[:, 3 * c + 1:3 * c + 2] * o_s
                 + gn[:, 3 * c + 2:3 * c + 3] * ow_ref[rows, :])
            o_ref[0, :, c * HEAD_DIM:(c + 1) * HEAD_DIM] = o[:, h * HEAD_DIM:(h + 1) * HEAD_DIM]


def _attn_sample(page_table, q, gn, kcvc, selk_new, selv_new, wink, winv, pages, past_len, ls):
    bs, n_pages = page_table.shape
    gp = min(SAMPLE_PAGES_PER_STEP, n_pages)
    n_groups = n_pages // gp
    tk = gp * 128
    blocks_per_step = tk // SEL_BLOCK
    assert LANES % blocks_per_step == 0 and tk % SAMPLE_TK == 0
    n_c = kcvc.shape[1]
    n_sel = -(-(past_len + ls) // SEL_BLOCK)
    n_sel_pad = -(-n_sel // LANES) * LANES
    csel = jnp.asarray(_cmp_to_sel(n_c, n_c - 1, n_sel_pad), BF16)
    key_blk = (np.arange(tk) // SEL_BLOCK)[None, :]
    onehot = [jnp.asarray((key_blk + s * blocks_per_step) % LANES == np.arange(LANES)[:, None], BF16)
              for s in range(LANES // blocks_per_step)]
    onehot = jnp.stack(onehot)
    n_rep = LANES // blocks_per_step
    nrow = N_Q_HEADS * ls
    kern = functools.partial(_attn_sample_kernel, n_groups=n_groups, gp=gp, past_len=past_len, ls=ls)
    per_seq = lambda *shape: pl.BlockSpec((1,) + shape, lambda b, j, pt: (b,) + (0,) * len(shape))
    const = lambda shape: pl.BlockSpec(shape, lambda b, j, pt: (0,) * len(shape))
    return pl.pallas_call(
        kern,
        grid_spec=pltpu.PrefetchScalarGridSpec(
            num_scalar_prefetch=1,
            grid=(bs, n_groups),
            in_specs=[per_seq(ls, D_ATTN), per_seq(ls, LANES), per_seq(n_c, 256),
                      const(csel.shape), const((D_ATTN, N_Q_HEADS * LANES)),
                      pl.BlockSpec((None, LANES, tk), lambda b, j, pt: (j % n_rep, 0, 0)),
                      per_seq(ls, LANES), per_seq(ls, LANES),
                      per_seq(wink.shape[1], LANES), per_seq(winv.shape[1], LANES),
                      pl.BlockSpec(memory_space=pl.ANY)],
            out_specs=per_seq(ls, D_ATTN),
            scratch_shapes=[pltpu.VMEM((2, gp, 2, N_KV_HEADS, HEAD_DIM, 128), F32),
                            pltpu.SemaphoreType.DMA((2,)),
                            pltpu.VMEM((nrow, LANES), BF16),
                            pltpu.VMEM((nrow, n_sel_pad), BF16),
                            pltpu.VMEM((nrow, LANES), F32),
                            pltpu.VMEM((nrow, LANES), F32),
                            pltpu.VMEM((nrow, LANES), F32),
                            pltpu.VMEM((nrow, LANES), F32),
                            pltpu.VMEM((nrow, LANES), F32)]),
        out_shape=jax.ShapeDtypeStruct((bs, ls, D_ATTN), F32),
        compiler_params=_cparams(("arbitrary", "arbitrary")),
        name="attn_sample",
    )(page_table, q, gn, kcvc, csel, _q_placement(), onehot, selk_new, selv_new, wink, winv, pages)


def _cmul(ar, ai, br, bi):
    return ar * br - ai * bi, ar * bi + ai * br


def _ssm_weights(lam_re, lam_im, log_dt, b_re, b_im, c_re, c_im, tc):
    dt = jnp.exp(log_dt)[:, None]
    mag = jnp.exp(lam_re * dt)
    a_re = mag * jnp.cos(lam_im * dt)
    a_im = mag * jnp.sin(lam_im * dt)
    den = lam_re * lam_re + lam_im * lam_im
    f_re = ((a_re - 1.0) * lam_re + a_im * lam_im) / den
    f_im = (a_im * lam_re - (a_re - 1.0) * lam_im) / den
    bb_re = f_re[..., None] * b_re - f_im[..., None] * b_im
    bb_im = f_re[..., None] * b_im + f_im[..., None] * b_re
    pr, pi = jnp.ones_like(a_re)[None], jnp.zeros_like(a_re)[None]
    sr, si = a_re, a_im
    while pr.shape[0] < tc + 1:
        nr, ni = _cmul(pr, pi, sr[None], si[None])
        pr, pi = jnp.concatenate([pr, nr]), jnp.concatenate([pi, ni])
        sr, si = _cmul(sr, si, sr, si)
    pr, pi = pr[:tc + 1], pi[:tc + 1]
    cpr = c_re[None] * pr[:, :, None, :] - c_im[None] * pi[:, :, None, :]
    cpi = c_re[None] * pi[:, :, None, :] + c_im[None] * pr[:, :, None, :]
    kker = jnp.einsum('tgop,gpi->gtoi', cpr[:tc], bb_re) - jnp.einsum('tgop,gpi->gtoi', cpi[:tc], bb_im)
    g = a_re.shape[0]
    nset = g // SSM_SET
    half = SSM_SET * SSM_STATE

    lag = np.arange(tc)[None, :] - np.arange(tc)[:, None]
    kpad = jnp.concatenate([kker, jnp.zeros_like(kker[:, :1])], axis=1)
    toep = kpad[:, np.where(lag >= 0, lag, tc)]
    t6 = jnp.transpose(toep, (0, 1, 4, 2, 3)).reshape(nset, SSM_SET, tc, SSM_GROUP, tc, SSM_GROUP)
    toep_c = jnp.transpose(t6, (0, 4, 2, 1, 3, 5)).reshape(nset, tc, tc * LANES, SSM_GROUP)
    qr, qi = pr[:tc][::-1], pi[:tc][::-1]
    n_re = qr[..., None] * bb_re[None] - qi[..., None] * bb_im[None]
    n_im = qr[..., None] * bb_im[None] + qi[..., None] * bb_re[None]
    n6 = jnp.transpose(jnp.stack([n_re, n_im], axis=2), (1, 0, 4, 2, 3))
    n6 = n6.reshape(nset, SSM_SET, tc, SSM_GROUP, 2, SSM_STATE)
    n_c = jnp.transpose(n6, (0, 4, 2, 1, 3, 5)).reshape(nset, 2, tc * LANES, SSM_STATE)
    m6 = jnp.transpose(jnp.stack([cpr[1:tc + 1], -cpi[1:tc + 1]], axis=3), (1, 3, 4, 0, 2))
    m6 = m6.reshape(nset, SSM_SET, 2, SSM_STATE, tc, SSM_GROUP)
    m_c = jnp.transpose(m6, (0, 4, 2, 1, 3, 5)).reshape(nset, tc, 2 * half, SSM_GROUP)
    return (toep_c.astype(BF16), n_c.astype(BF16), m_c.astype(BF16),
            pr[tc].reshape(1, nset * half), pi[tc].reshape(1, nset * half))


def _expand_blockdiag(c_ref, w_scr, rows_per_group):
    _, nblk, rows, unit = c_ref.shape
    width = SSM_SET * unit
    rep = (lax.broadcasted_iota(jnp.int32, (unit, width), 1) % unit
           == lax.broadcasted_iota(jnp.int32, (unit, width), 0))
    rep = jnp.where(rep, 1.0, 0.0).astype(BF16)
    row_group = (lax.broadcasted_iota(jnp.int32, (rows, width), 0) // rows_per_group) % SSM_SET
    mask = row_group == lax.broadcasted_iota(jnp.int32, (rows, width), 1) // unit
    for b in range(nblk):
        w_scr[:, b * width:(b + 1) * width] = jnp.where(mask, _dot(c_ref[0, b], rep), 0.0).astype(BF16)


def _chunk_rows(u_ref, tc):
    n = u_ref.shape[0] // tc
    return jnp.concatenate([u_ref[pl.ds(s, n, stride=tc), :] for s in range(tc)], axis=1)


def _ssm_state_kernel(u_ref, n_ref, s_ref, wn_scr, *, tc):
    @pl.when(pl.program_id(1) == 0)
    def _():
        _expand_blockdiag(n_ref, wn_scr, SSM_GROUP)

    s_ref[...] = _dot(_chunk_rows(u_ref, tc).astype(BF16), wn_scr[...])


def _ssm_carry_kernel(s_ref, h0_ref, are_ref, aim_ref, hs_ref, f_ref, h_scr, *, batch, nset):
    half = SSM_SET * SSM_STATE

    @pl.when(pl.program_id(0) == 0)
    def _():
        h_scr[...] = h0_ref[...]

    def body(k, h):
        sk = jnp.concatenate([s_ref[b, pl.ds(k, 1), :] for b in range(batch)], axis=0)
        for b in range(batch):
            hs_ref[b, pl.ds(k, 1), :] = h[b:b + 1]
        out = []
        for j in range(nset):
            c0 = 2 * j * half
            re, im = h[:, c0:c0 + half], h[:, c0 + half:c0 + 2 * half]
            ar, ai = are_ref[:, j * half:(j + 1) * half], aim_ref[:, j * half:(j + 1) * half]
            out.append(ar * re - ai * im + sk[:, c0:c0 + half])
            out.append(ar * im + ai * re + sk[:, c0 + half:c0 + 2 * half])
        return jnp.concatenate(out, axis=1)

    h = lax.fori_loop(0, s_ref.shape[1], body, h_scr[...])
    h_scr[...] = h
    f_ref[...] = h


def _ssm_out_kernel(u_ref, t_ref, m_ref, h_ref, y_ref, wt_scr, wm_scr, *, tc):
    @pl.when(pl.program_id(1) == 0)
    def _():
        _expand_blockdiag(t_ref, wt_scr, SSM_GROUP)
        _expand_blockdiag(m_ref, wm_scr, SSM_STATE)

    y = _dot(_chunk_rows(u_ref, tc).astype(BF16), wt_scr[...]) + _dot(h_ref[...].astype(BF16), wm_scr[...])
    n = y.shape[0]
    for s in range(tc):
        y_ref[pl.ds(s, n, stride=tc), :] = y[:, s * LANES:(s + 1) * LANES]


def _ssm(u2d, batch, seq_len, tc, h0_re, h0_im, ssm_w):
    toep, nmat, mmat, at_re, at_im = ssm_w
    nset = toep.shape[0]
    half = SSM_SET * SSM_STATE
    ncol = 2 * nset * half
    t_tok = batch * seq_len
    nck = seq_len // tc
    tm = min(2048, t_tok)
    rt = tm // tc
    grid = (nset, t_tok // tm)
    u_spec = pl.BlockSpec((tm, LANES), lambda j, i: (i, j))
    w_spec = lambda w: pl.BlockSpec((1,) + w.shape[1:], lambda j, i: (j, 0, 0, 0))
    st_spec = pl.BlockSpec((rt, 2 * half), lambda j, i: (i, j))
    kk = tc * LANES
    s = pl.pallas_call(
        functools.partial(_ssm_state_kernel, tc=tc),
        grid=grid,
        in_specs=[u_spec, w_spec(nmat)],
        out_specs=st_spec,
        out_shape=jax.ShapeDtypeStruct((batch * nck, ncol), F32),
        scratch_shapes=[pltpu.VMEM((kk, 2 * half), BF16)],
        compiler_params=_cparams(("parallel", "arbitrary")),
        name="ssm_state",
    )(u2d, nmat)
    kb = min(128, nck)
    h0 = jnp.stack([h0_re.reshape(batch, nset, half), h0_im.reshape(batch, nset, half)], axis=2).reshape(batch, ncol)
    seq3 = pl.BlockSpec((batch, kb, ncol), lambda c: (0, c, 0))
    hs, f = pl.pallas_call(
        functools.partial(_ssm_carry_kernel, batch=batch, nset=nset),
        grid=(nck // kb,),
        in_specs=[seq3, _full((batch, ncol)), _full(at_re.shape), _full(at_im.shape)],
        out_specs=[seq3, _full((batch, ncol))],
        out_shape=[jax.ShapeDtypeStruct((batch, nck, ncol), F32), jax.ShapeDtypeStruct((batch, ncol), F32)],
        scratch_shapes=[pltpu.VMEM((batch, ncol), F32)],
        compiler_params=_cparams(("arbitrary",)),
        name="ssm_carry",
    )(s.reshape(batch, nck, ncol), h0, at_re, at_im)
    y = pl.pallas_call(
        functools.partial(_ssm_out_kernel, tc=tc),
        grid=grid,
        in_specs=[u_spec, w_spec(toep), w_spec(mmat), st_spec],
        out_specs=u_spec,
        out_shape=jax.ShapeDtypeStruct(u2d.shape, F32),
        scratch_shapes=[pltpu.VMEM((kk, kk), BF16), pltpu.VMEM((2 * half, kk), BF16)],
        compiler_params=_cparams(("parallel", "arbitrary")),
        name="ssm_out",
    )(u2d, toep, mmat, hs.reshape(batch * nck, ncol))
    f = f.reshape(batch, nset, 2, SSM_SET, SSM_STATE)
    g = nset * SSM_SET
    return y, f[:, :, 0].reshape(batch, g, SSM_STATE), f[:, :, 1].reshape(batch, g, SSM_STATE)


def _post_kernel(x_ref, oa_ref, ys_ref, u_ref, g_ref, fg_ref, d_ref, bglu_ref,
                 wza_ref, wzs_ref, wgm_ref, wglu_ref, wla_ref, wls_ref, wo_ref, out_ref):
    x = x_ref[...]
    d = x.shape[1]
    xn = x * lax.rsqrt(jnp.mean(x * x, axis=-1, keepdims=True) + RMS_EPS) * g_ref[...]
    xb = xn.astype(BF16)
    z_a = _dot(xb, wza_ref[...])
    z_s = _dot(xb, wzs_ref[...])
    gm = jax.nn.sigmoid(_dot(xb, wgm_ref[...]))
    branch_a = _dot((oa_ref[...] * jax.nn.silu(z_a)).astype(BF16), wla_ref[...])
    y = jax.nn.gelu(ys_ref[...] + d_ref[...] * u_ref[...])
    y = y * jax.nn.sigmoid(_dot(y.astype(BF16), wglu_ref[...]) + bglu_ref[...])
    branch_b = _dot((y * jax.nn.silu(z_s)).astype(BF16), wls_ref[...])
    merged = gm[:, :d] * branch_a + gm[:, d:] * branch_b
    r = x + _dot(merged.astype(BF16), wo_ref[...])
    out_ref[...] = r * lax.rsqrt(jnp.mean(r * r, axis=-1, keepdims=True) + RMS_EPS) * fg_ref[...]


def _post(x2d, o_attn, y_ssm, u, norm_g, final_g, ssm_d, b_glu, wza, wzs, wgm, wglu, wla, wls, wo):
    t, d = x2d.shape
    tm = min(TOKEN_TILE, t)
    row = lambda n: pl.BlockSpec((tm, n), lambda i: (i, 0))
    ws = [wza, wzs, wgm, wglu, wla, wls, wo]
    vecs = [norm_g.reshape(1, d), final_g.reshape(1, d), ssm_d.reshape(1, -1), b_glu.reshape(1, -1)]
    return pl.pallas_call(
        _post_kernel,
        grid=(t // tm,),
        in_specs=[row(d), row(o_attn.shape[1]), row(y_ssm.shape[1]), row(u.shape[1])]
                 + [_full(v.shape) for v in vecs] + [_full(w.shape) for w in ws],
        out_specs=row(d),
        out_shape=jax.ShapeDtypeStruct((t, d), F32),
        compiler_params=_cparams(("parallel",)),
        name="post",
    )(x2d, o_attn, y_ssm, u, *vecs, *ws)


SSM_CHUNK = 8


def kernel(x_prompt, x_sample, cache_kv, cache_win_kv, state_ssm_re, state_ssm_im, page_table, norm_g, w_in, cmp_pe, cmp_w1, cmp_b1, cmp_w2, ssm_lam_re, ssm_lam_im, ssm_log_dt, ssm_b_re, ssm_b_im, ssm_c_re, ssm_c_im, ssm_d, w_glu, b_glu, w_lift_attn, w_lift_ssm, w_out, final_g):
    depth = norm_g.shape[0]
    assert depth == 1
    l = 0
    bp, lp, d = x_prompt.shape
    bs, ls, _ = x_sample.shape
    n_pages, page = page_table.shape[1], cache_kv.shape[2]
    past_len = n_pages * page
    w_buf = cache_win_kv.shape[2]
    d_ssm = ssm_d.shape[1]
    n_grp = d_ssm // SSM_GROUP
    assert page == 128 and w_buf == WINDOW and lp % Q_BLOCK == 0 and lp >= WINDOW

    splits = (D_ATTN, N_KV_SLOTS * N_KV_HEADS * HEAD_DIM, 3 * N_Q_HEADS, D_ATTN, d_ssm, d_ssm, 2 * d)
    offs = np.concatenate([[0], np.cumsum(splits)])
    wb = w_in[l].astype(BF16)
    wq, wkv, wg, wza, wu, wzs, wgm = [wb[:, offs[k]:offs[k + 1]] for k in range(7)]
    wg = jnp.pad(wg, ((0, 0), (0, LANES - wg.shape[1])))
    cw1, cb, cw2 = _compress_weights(cmp_pe[l], cmp_w1[l], cmp_b1[l], cmp_w2[l])
    ssm_p = (ssm_lam_re[l], ssm_lam_im[l], ssm_log_dt[l], ssm_b_re[l], ssm_b_im[l], ssm_c_re[l], ssm_c_im[l])
    post_w = (norm_g[l], final_g, ssm_d[l], b_glu[l], wza, wzs, wgm, w_glu[l].astype(BF16),
              w_lift_attn[l].astype(BF16), w_lift_ssm[l].astype(BF16), w_out[l].astype(BF16))

    xp = x_prompt.reshape(bp * lp, d)
    q, kv_t, win_t, cmp_rows, selk, selv, wink, winv, gn, u = _pre(xp, norm_g[l], wq, wkv, wg, wu, lp, True)
    ident = jnp.arange(bp * (lp // 128), dtype=jnp.int32).reshape(bp, lp // 128)
    kcvc = _compress(ident, cmp_rows.reshape(bp * lp // 128, 128, 2 * LANES), cw1, cb, cw2)
    o_attn = _attn_prompt(q, gn, selk, selv, wink, winv, kcvc, bp, lp)
    assert lp % SSM_CHUNK == 0 and ls == SSM_CHUNK
    ssm_w = _ssm_weights(*ssm_p, SSM_CHUNK)
    h0 = jnp.zeros((bp, n_grp, SSM_STATE), F32)
    y_ssm, hr_p, hi_p = _ssm(u, bp, lp, SSM_CHUNK, h0, h0, ssm_w)
    y_prompt = _post(xp, o_attn, y_ssm, u, *post_w).reshape(bp, lp, d)
    kv_prompt = jnp.transpose(kv_t.reshape(bp, 4, N_KV_HEADS, HEAD_DIM, lp), (0, 4, 1, 2, 3))[None]
    win_t = win_t.reshape(bp, 2, N_KV_HEADS, HEAD_DIM, lp)[..., lp - WINDOW:]
    win_prompt = jnp.transpose(win_t, (0, 4, 1, 2, 3))[None]

    xs = x_sample.reshape(bs * ls, d)
    q, kv, win, _, selk, selv, wink, winv, gn, u = _pre(xs, norm_g[l], wq, wkv, wg, wu, ls, False)
    pages = jnp.transpose(cache_kv[l], (0, 2, 3, 4, 1))
    kcvc = _compress(page_table, pages, cw1, cb, cw2)
    cwin = cache_win_kv[l].reshape(bs, w_buf, 2 * LANES)
    wk_all = jnp.concatenate([cwin[:, :, :LANES].astype(BF16), wink.reshape(bs, ls, LANES)], axis=1)
    wv_all = jnp.concatenate([cwin[:, :, LANES:].astype(BF16), winv.reshape(bs, ls, LANES)], axis=1)
    o_attn = _attn_sample(page_table, q.reshape(bs, ls, D_ATTN), gn.reshape(bs, ls, LANES), kcvc,
                          selk[:, :LANES].reshape(bs, ls, LANES), selv.reshape(bs, ls, LANES),
                          wk_all, wv_all, pages, past_len, ls)
    y_ssm, hr_s, hi_s = _ssm(u, bs, ls, SSM_CHUNK, state_ssm_re[l], state_ssm_im[l], ssm_w)
    y_sample = _post(xs, o_attn.reshape(bs * ls, D_ATTN), y_ssm, u, *post_w).reshape(bs, ls, d)
    kv_sample = kv.reshape(1, bs, ls, 4, N_KV_HEADS, HEAD_DIM)
    win_new = win.reshape(bs, ls, 2, N_KV_HEADS, HEAD_DIM)
    win_sample = jnp.concatenate([cache_win_kv[l], win_new], axis=1)[None, :, ls:]

    return (y_prompt, y_sample, kv_prompt, win_prompt, hr_p[None], hi_p[None],
            kv_sample, win_sample, hr_s[None], hi_s[None])
```
